```python
import jax, jax.numpy as jnp
from jax import lax
import numpy as np

D_MODEL = 1024
BATCH = 32
SEQ = 2048
DEPTH = 2

N_EVEN = (DEPTH + 1) // 2
N_ODD = DEPTH // 2
D_MIX = D_MODEL

A_DK = 128
A_DV = 128
A_WIDTH = D_MIX // 2
A_HEADS = A_WIDTH // A_DK
A_IN = 5 * A_WIDTH
HGRN_CHUNK = 32
B_DH = 64
B_WIDTH = D_MIX - A_WIDTH
B_HEADS = B_WIDTH // B_DH
DECAY_LORA = 64
AAA_LORA = 64
GATE_LORA = 128
B_SPLIT = (B_WIDTH, B_WIDTH, B_WIDTH, DECAY_LORA, DECAY_LORA, AAA_LORA, AAA_LORA, GATE_LORA)
B_IN = sum(B_SPLIT)
AB_IN = A_IN + B_IN
RWKV_GN_EPS = 64e-5
C_WIDTH = D_MIX // 2
C_GROUPS = 4
C_DG = C_WIDTH // C_GROUPS
D_WIDTH = D_MIX - C_WIDTH
CONV_K = 31
CD_IN = C_WIDTH + 2 * D_WIDTH
D_FF = 2816
FFN_CONV_K = 3
RMS_EPS = 1e-6
LN_EPS = 1e-5

kernel_name = 'hybrid_hgrn2_rwkv7_fnet_conformer_encoder'


def _split(t, sizes):
    offsets = np.cumsum(sizes)[:-1].tolist()
    return jnp.split(t, offsets, axis=-1)


def _rmsnorm(x, g, eps=RMS_EPS):
    x32 = x.astype(jnp.float32)
    y = x32 * lax.rsqrt(jnp.mean(x32 * x32, axis=-1, keepdims=True) + eps)
    return (y * g.astype(jnp.float32)).astype(x.dtype)


def _layernorm(x, g, b, eps=LN_EPS):
    x32 = x.astype(jnp.float32)
    mu = jnp.mean(x32, axis=-1, keepdims=True)
    xc = x32 - mu
    y = xc * lax.rsqrt(jnp.mean(xc * xc, axis=-1, keepdims=True) + eps)
    return (y * g.astype(jnp.float32) + b.astype(jnp.float32)).astype(x.dtype)


def _adaln(c, w, b):
    mod = jax.nn.silu(c) @ w + b
    shift, scale, gate = jnp.split(mod, 3, axis=-1)
    return shift[:, None, :], scale[:, None, :], gate[:, None, :]


def _dwconv(x, w, b):
    k_width, ch = w.shape
    y = lax.conv_general_dilated(
        x, w[:, None, :].astype(x.dtype), window_strides=(1,),
        padding=[(k_width // 2, k_width // 2)],
        dimension_numbers=('NWC', 'WIO', 'NWC'), feature_group_count=ch)
    return y + b


def _gla_chunk_forward(q, k, v, log_f):
    bsz, seq, heads, dk = q.shape
    dv = v.shape[-1]
    n_chunks = seq // HGRN_CHUNK

    def blocks(t):
        return t.astype(jnp.float32).reshape(bsz, n_chunks, HGRN_CHUNK, heads, t.shape[-1])

    q, k, v, log_f = blocks(q), blocks(k), blocks(v), blocks(log_f)
    cum = jnp.cumsum(log_f, axis=2)
    cum_last = cum[:, :, -1:]
    q_dec = q * jnp.exp(cum)
    k_inv = k * jnp.exp(-cum)
    k_end = k * jnp.exp(cum_last - cum)
    tri = jnp.tril(jnp.ones((HGRN_CHUNK, HGRN_CHUNK), dtype=bool))
    scores = jnp.einsum('bnchd,bnshd->bnhcs', q_dec, k_inv)
    scores = jnp.where(tri, scores, 0.0)
    o_intra = jnp.einsum('bnhcs,bnshv->bnchv', scores, v)

    def step(state, xs):
        q_n, k_n, v_n, dec_n = xs
        o_n = jnp.einsum('bchd,bhdv->bchv', q_n, state)
        state = state * dec_n[..., None] + jnp.einsum('bchd,bchv->bhdv', k_n, v_n)
        return state, o_n

    xs = (jnp.moveaxis(q_dec, 1, 0), jnp.moveaxis(k_end, 1, 0), jnp.moveaxis(v, 1, 0),
          jnp.moveaxis(jnp.exp(cum_last[:, :, 0]), 1, 0))
    state0 = jnp.zeros((bsz, heads, dk, dv), jnp.float32)
    _, o_inter = lax.scan(step, state0, xs)
    return (o_intra + jnp.moveaxis(o_inter, 0, 1)).reshape(bsz, seq, heads, dv)


def _hgrn2_mixer(q, f_fwd_raw, f_bwd_raw, i, g, lb, norm_g):
    bsz, seq, _ = q.shape
    heads = lambda t: t.reshape(bsz, seq, A_HEADS, A_DK)
    lb = lb.reshape(A_HEADS, A_DK)

    def gates(f_raw):
        f = lb + (1.0 - lb) * jax.nn.sigmoid(heads(f_raw).astype(jnp.float32))
        return 1.0 - f, jnp.log(f)

    k_f, lf_f = gates(f_fwd_raw)
    k_b, lf_b = gates(f_bwd_raw)
    q4, i4 = heads(q), heads(i)
    flip = lambda t: jnp.flip(t, axis=1)
    o = _gla_chunk_forward(q4, k_f, i4, lf_f) + flip(
        _gla_chunk_forward(flip(q4), flip(k_b), flip(i4), flip(lf_b)))
    o = _rmsnorm(o, norm_g) * jax.nn.silu(heads(g).astype(jnp.float32))
    return o.reshape(bsz, seq, A_WIDTH).astype(q.dtype)


def _rwkv7_scan(r, decay, k, v, a_vec, b_vec):
    bsz, _, heads, n = r.shape
    xs = tuple(jnp.moveaxis(t.astype(jnp.float32), 1, 0) for t in (r, decay, k, v, a_vec, b_vec))

    def step(state, inp):
        r_t, w_t, k_t, v_t, a_t, b_t = inp
        sa = jnp.einsum('bhvk,bhk->bhv', state, a_t)
        state = (state * w_t[:, :, None, :] + sa[..., None] * b_t[:, :, None, :]
                 + v_t[..., None] * k_t[:, :, None, :])
        return state, jnp.einsum('bhvk,bhk->bhv', state, r_t)

    state0 = jnp.zeros((bsz, heads, n, n), jnp.float32)
    _, ys = lax.scan(step, state0, xs)
    return jnp.moveaxis(ys, 0, 1)


def _rwkv7_mixer(p, mu, w0, w2, a0, a2, g2, k_k, k_a, r_k, lnx_w, lnx_b):
    bsz, seq, _ = p.shape
    p_prev = jnp.pad(p[:, :-1], ((0, 0), (1, 0), (0, 0)))
    p_next = jnp.pad(p[:, 1:], ((0, 0), (0, 1), (0, 0)))
    p = p + mu[0] * (p_prev - p) + mu[1] * (p_next - p)
    r, k, v, wd_f, wd_b, ad_f, ad_b, gd = _split(p, B_SPLIT)
    heads = lambda t: t.astype(jnp.float32).reshape(bsz, seq, B_HEADS, B_DH)
    g = jax.nn.sigmoid(gd) @ g2
    kk = heads(k * k_k)
    kk = kk * lax.rsqrt(jnp.sum(kk * kk, axis=-1, keepdims=True) + 1e-12)
    r_h, k_h, v_h = heads(r), heads(k), heads(v)
    k_a_h = k_a.astype(jnp.float32).reshape(B_HEADS, B_DH)
    r_k32 = r_k.astype(jnp.float32)
    flip = lambda t: jnp.flip(t, axis=1)
    y_wkv = jnp.zeros_like(r_h)
    bonus = jnp.zeros_like(r_h)
    for d, (wd, ad, reverse) in enumerate(((wd_f, ad_f, False), (wd_b, ad_b, True))):
        w = -jax.nn.softplus(-(w0[d] + jnp.tanh(wd) @ w2[d])) - 0.5
        decay = jnp.exp(-jnp.exp(heads(w)))
        a = jax.nn.sigmoid(heads(a0[d] + ad @ a2[d]))
        k_d = k_h * (1.0 + (a - 1.0) * k_a_h)
        args = (r_h, decay, k_d, v_h, -kk, kk * a)
        if reverse:
            y_wkv = y_wkv + flip(_rwkv7_scan(*[flip(t) for t in args]))
        else:
            y_wkv = y_wkv + _rwkv7_scan(*args)
        bonus = bonus + jnp.sum(r_h * k_d * r_k32, axis=-1, keepdims=True) * v_h
    mean = jnp.mean(y_wkv, axis=-1, keepdims=True)
    yc = y_wkv - mean
    y = yc * lax.rsqrt(jnp.mean(yc * yc, axis=-1, keepdims=True) + RWKV_GN_EPS)
    y = y.reshape(bsz, seq, B_WIDTH) * lnx_w + lnx_b
    y = (y + bonus.reshape(bsz, seq, B_WIDTH)) * g
    return y.astype(p.dtype)


def _fourier_mix(u):
    bsz, seq, _ = u.shape
    u4 = u.astype(jnp.float32).reshape(bsz, seq, C_GROUPS, C_DG)
    y = jnp.fft.fftn(u4, axes=(1, 3), norm='ortho').real
    return y.reshape(bsz, seq, C_WIDTH).astype(u.dtype)


def _conformer_conv(u, conv_w, conv_b, ln_g, ln_b):
    val, gate = jnp.split(u, 2, axis=-1)
    h = val * jax.nn.sigmoid(gate)
    h = _dwconv(h, conv_w, conv_b)
    return jax.nn.silu(_layernorm(h, ln_g, ln_b))


def _conv_ffn(h, w_up, conv_w, conv_b, w_down):
    u, v = jnp.split(h @ w_up, 2, axis=-1)
    u = _dwconv(u, conv_w, conv_b)
    return (jax.nn.silu(u) * v) @ w_down


def setup_inputs(seed: int = 0) -> dict:
    key = jax.random.key(seed)
    ks = iter(jax.random.split(key, 40))
    f32 = jnp.float32

    def nrm(shape, scale):
        return scale * jax.random.normal(next(ks), shape, f32)

    D = D_MODEL
    return {
        'x': nrm((BATCH, SEQ, D), 1.0),
        'c': nrm((BATCH, D), 1.0),
        'ada_w': nrm((DEPTH, 2, D, 3 * D), 0.5 * D ** -0.5),
        'ada_b': nrm((DEPTH, 2, 3 * D), 0.01),
        'norm_g': 1.0 + nrm((DEPTH, 2, D), 0.02),
        'final_g': 1.0 + nrm((D,), 0.02),
        'ab_w_in': nrm((N_EVEN, D, AB_IN), D ** -0.5),
        'ab_w_out': nrm((N_EVEN, D_MIX, D), D_MIX ** -0.5),
        'hgrn_gamma': nrm((DEPTH + 1, A_WIDTH), 0.1),
        'hgrn_norm_g': 1.0 + nrm((N_EVEN, A_DV), 0.02),
        'rwkv_mu': jax.random.uniform(next(ks), (N_EVEN, 2, B_IN), f32, 0.0, 0.5),
        'rwkv_w0': nrm((N_EVEN, 2, B_WIDTH), 0.5),
        'rwkv_w2': nrm((N_EVEN, 2, DECAY_LORA, B_WIDTH), 0.1 * DECAY_LORA ** -0.5),
        'rwkv_a0': nrm((N_EVEN, 2, B_WIDTH), 0.1),
        'rwkv_a2': nrm((N_EVEN, 2, AAA_LORA, B_WIDTH), AAA_LORA ** -0.5),
        'rwkv_g2': nrm((N_EVEN, GATE_LORA, B_WIDTH), GATE_LORA ** -0.5),
        'rwkv_kk': 0.85 + nrm((N_EVEN, B_WIDTH), 0.05),
        'rwkv_ka': 1.0 + nrm((N_EVEN, B_WIDTH), 0.05),
        'rwkv_rk': nrm((N_EVEN, B_HEADS, B_DH), 0.1),
        'rwkv_lnx_w': 1.0 + nrm((N_EVEN, B_WIDTH), 0.02),
        'rwkv_lnx_b': nrm((N_EVEN, B_WIDTH), 0.01),
        'cd_w_in': nrm((N_ODD, D, CD_IN), D ** -0.5),
        'cd_w_out': nrm((N_ODD, D_MIX, D), D_MIX ** -0.5),
        'dconv_w': nrm((N_ODD, CONV_K, D_WIDTH), CONV_K ** -0.5),
        'dconv_b': nrm((N_ODD, D_WIDTH), 0.01),
        'dconv_ln_g': 1.0 + nrm((N_ODD, D_WIDTH), 0.02),
        'dconv_ln_b': nrm((N_ODD, D_WIDTH), 0.01),
        'ffn_w_up': nrm((DEPTH, D, 2 * D_FF), D ** -0.5),
        'ffn_conv_w': nrm((DEPTH, FFN_CONV_K, D_FF), FFN_CONV_K ** -0.5),
        'ffn_conv_b': nrm((DEPTH, D_FF), 0.01),
        'ffn_w_down': nrm((DEPTH, D_FF, D), D_FF ** -0.5),
    }


def reference(x, c, ada_w, ada_b, norm_g, final_g, ab_w_in, ab_w_out, hgrn_gamma, hgrn_norm_g,
              rwkv_mu, rwkv_w0, rwkv_w2, rwkv_a0, rwkv_a2, rwkv_g2, rwkv_kk, rwkv_ka, rwkv_rk,
              rwkv_lnx_w, rwkv_lnx_b, cd_w_in, cd_w_out, dconv_w, dconv_b, dconv_ln_g, dconv_ln_b,
              ffn_w_up, ffn_conv_w, ffn_conv_b, ffn_w_down):
    lower_bounds = jnp.cumsum(jax.nn.softmax(hgrn_gamma.astype(jnp.float32), axis=0), axis=0)
    for l in range(DEPTH):
        j = l // 2
        shift, scale, gate = _adaln(c, ada_w[l, 0], ada_b[l, 0])
        h = _rmsnorm(x, norm_g[l, 0]) * (1.0 + scale) + shift
        if l % 2 == 0:
            p = h @ ab_w_in[j]
            pa, pb = p[..., :A_IN], p[..., A_IN:]
            q, f_fwd, f_bwd, i, g = jnp.split(pa, 5, axis=-1)
            y_a = _hgrn2_mixer(q, f_fwd, f_bwd, i, g, lower_bounds[l], hgrn_norm_g[j])
            y_b = _rwkv7_mixer(pb, rwkv_mu[j], rwkv_w0[j], rwkv_w2[j], rwkv_a0[j], rwkv_a2[j],
                               rwkv_g2[j], rwkv_kk[j], rwkv_ka[j], rwkv_rk[j],
                               rwkv_lnx_w[j], rwkv_lnx_b[j])
            mix = jnp.concatenate([y_a, y_b], axis=-1) @ ab_w_out[j]
        else:
            p = h @ cd_w_in[j]
            y_c = _fourier_mix(p[..., :C_WIDTH])
            y_d = _conformer_conv(p[..., C_WIDTH:], dconv_w[j], dconv_b[j],
                                  dconv_ln_g[j], dconv_ln_b[j])
            mix = jnp.concatenate([y_c, y_d], axis=-1) @ cd_w_out[j]
        x = x + gate * mix
        shift, scale, gate = _adaln(c, ada_w[l, 1], ada_b[l, 1])
        h = _rmsnorm(x, norm_g[l, 1]) * (1.0 + scale) + shift
        x = x + gate * _conv_ffn(h, ffn_w_up[l], ffn_conv_w[l], ffn_conv_b[l], ffn_w_down[l])
    return _rmsnorm(x, final_g)
```

```python
import functools

import numpy as np
import jax
import jax.numpy as jnp
from jax import lax
from jax.experimental import pallas as pl
from jax.experimental.pallas import tpu as pltpu

F32 = jnp.float32
BF16 = jnp.bfloat16
HI = lax.Precision.HIGHEST

LANES = 128
SUBLANES = 8
VMEM_LIMIT_BYTES = 56 * 1024 * 1024

RMS_EPS = 1e-6
LN_EPS = 1e-5
RWKV_GN_EPS = 64e-5
RWKV_DH = 64
CONV_K = 31
CONV_PAD = 16
HGRN_C = 64
RWKV_C = 128
FF_CHUNK = 256
HALO = 2 * SUBLANES

NN = (((1,), (0,)), ((), ()))
NT = (((1,), (1,)), ((), ()))
TN = (((0,), (0,)), ((), ()))


def _mmf(a, b, dims=NN):
    return lax.dot_general(a, b, dims, precision=HI, preferred_element_type=F32)


def _mmb(a, b, dims=NN):
    return lax.dot_general(a.astype(BF16), b.astype(BF16), dims, preferred_element_type=F32)


def _sigmoid(x):
    return jax.nn.sigmoid(x)


def _silu(x):
    return x * jax.nn.sigmoid(x)


def _cparams(sem):
    return pltpu.CompilerParams(dimension_semantics=sem, vmem_limit_bytes=VMEM_LIMIT_BYTES)


def _const_spec(shape):
    nd = len(shape)
    return pl.BlockSpec(shape, lambda *_: (0,) * nd, pipeline_mode=pl.Buffered(1))


def _adaln_body(c_ref, w_ref, b_ref, o_ref):
    o_ref[0] = _mmf(_silu(c_ref[...]), w_ref[0]) + b_ref[0]


def _adaln(c, ada_w, ada_b):
    k, d, d3 = ada_w.shape
    b = c.shape[0]
    return pl.pallas_call(
        _adaln_body,
        grid=(k, d3 // d),
        in_specs=[pl.BlockSpec((b, d), lambda i, j: (0, 0)),
                  pl.BlockSpec((1, d, d), lambda i, j: (i, 0, j)),
                  pl.BlockSpec((1, 1, d), lambda i, j: (i, 0, j))],
        out_specs=pl.BlockSpec((1, b, d), lambda i, j: (i, 0, j)),
        out_shape=jax.ShapeDtypeStruct((k, b, d3), F32),
        compiler_params=_cparams(("parallel", "parallel")),
        name="adaln",
    )(c, ada_w, ada_b.reshape(k, 1, d3))


def _modulated_norm(x, g, scale, shift):
    ms = jnp.mean(x * x, axis=-1, keepdims=True)
    return (x * lax.rsqrt(ms + RMS_EPS) * g) * (1.0 + scale) + shift


def _inproj_body(x_ref, g_ref, sc_ref, sh_ref, w_ref, *o_refs):
    hb = _modulated_norm(x_ref[0], g_ref[...], sc_ref[0], sh_ref[0]).astype(BF16)
    off = 0
    for o_ref in o_refs:
        n = o_ref.shape[-1]
        o_ref[0] = jnp.dot(hb, w_ref[:, off:off + n], preferred_element_type=F32).astype(o_ref.dtype)
        off += n


def _inproj(x, g, scale, shift, w, splits, tm):
    b, s, d = x.shape
    n = w.shape[1]
    assert sum(splits) == n and s % tm == 0
    row = pl.BlockSpec((1, 1, d), lambda i, j: (i, 0, 0))
    return pl.pallas_call(
        _inproj_body,
        grid=(b, s // tm),
        in_specs=[pl.BlockSpec((1, tm, d), lambda i, j: (i, j, 0)),
                  pl.BlockSpec((1, d), lambda i, j: (0, 0)),
                  row, row,
                  _const_spec((d, n))],
        out_specs=[pl.BlockSpec((1, tm, m), lambda i, j: (i, j, 0)) for m in splits],
        out_shape=[jax.ShapeDtypeStruct((b, s, m), F32) for m in splits],
        compiler_params=_cparams(("parallel", "parallel")),
        name="inproj",
    )(x, g.reshape(1, d), scale, shift, w.astype(BF16))


def _hgrn_body(q_ref, ff_ref, fb_ref, i_ref, g_ref, gam_ref, ng_ref, o_ref,
               of_ref, ob_ref, sf_ref, sb_ref, *, seq, layer):
    c = HGRN_C
    n_chunks = seq // c
    rows = [gam_ref[pl.ds(i, 1), :] for i in range(gam_ref.shape[0])]
    mx = functools.reduce(jnp.maximum, rows)
    es = [jnp.exp(r - mx) for r in rows]
    lb = sum(es[:layer + 1]) / sum(es)

    r_i = lax.broadcasted_iota(jnp.int32, (c, c), 0)
    c_i = lax.broadcasted_iota(jnp.int32, (c, c), 1)
    lower = c_i <= r_i
    upper = c_i >= r_i
    sf_ref[...] = jnp.zeros_like(sf_ref)
    sb_ref[...] = jnp.zeros_like(sb_ref)

    def chunk(t0, fraw_ref, mask, st_ref, out_ref, tot_row):
        q = q_ref[0, pl.ds(t0, c), :]
        v = i_ref[0, pl.ds(t0, c), :]
        f = lb + (1.0 - lb) * _sigmoid(fraw_ref[0, pl.ds(t0, c), :])
        k = 1.0 - f
        cum = _mmf(mask.astype(F32), jnp.log(f))
        mid = cum[c // 2:c // 2 + 1, :]
        tot = cum[tot_row:tot_row + 1, :]
        scores = _mmb(q * jnp.exp(cum - mid), k * jnp.exp(mid - cum), NT)
        scores = jnp.where(mask, scores, 0.0)
        st = st_ref[...]
        out_ref[pl.ds(t0, c), :] = _mmb(scores, v) + _mmb(q * jnp.exp(cum), st, NT)
        st_ref[...] = st * jnp.exp(tot) + _mmb(v, k * jnp.exp(tot - cum), TN)

    def body(i, carry):
        chunk(pl.multiple_of(i * c, c), ff_ref, lower, sf_ref, of_ref, c - 1)
        chunk(pl.multiple_of((n_chunks - 1 - i) * c, c), fb_ref, upper, sb_ref, ob_ref, 0)
        return carry

    lax.fori_loop(0, n_chunks, body, 0)

    def epilogue(i, carry):
        t0 = pl.multiple_of(i * c, c)
        o = of_ref[pl.ds(t0, c), :] + ob_ref[pl.ds(t0, c), :]
        ms = jnp.mean(o * o, axis=-1, keepdims=True)
        y = o * lax.rsqrt(ms + RMS_EPS) * ng_ref[...]
        o_ref[0, pl.ds(t0, c), :] = y * _silu(g_ref[0, pl.ds(t0, c), :])
        return carry

    lax.fori_loop(0, n_chunks, epilogue, 0)


def _hgrn(pa, gamma, norm_g, layer):
    b, s, w5 = pa.shape
    heads = w5 // 5 // LANES
    col = lambda k: pl.BlockSpec((1, s, LANES), lambda i, h, k=k: (i, 0, k * heads + h))
    return pl.pallas_call(
        functools.partial(_hgrn_body, seq=s, layer=layer),
        grid=(b, heads),
        in_specs=[col(0), col(1), col(2), col(3), col(4),
                  pl.BlockSpec((gamma.shape[0], LANES), lambda i, h: (0, h)),
                  pl.BlockSpec((1, LANES), lambda i, h: (0, 0))],
        out_specs=pl.BlockSpec((1, s, LANES), lambda i, h: (i, 0, h)),
        out_shape=jax.ShapeDtypeStruct((b, s, heads * LANES), F32),
        scratch_shapes=[pltpu.VMEM((s, LANES), F32), pltpu.VMEM((s, LANES), F32),
                        pltpu.VMEM((LANES, LANES), F32), pltpu.VMEM((LANES, LANES), F32)],
        compiler_params=_cparams(("parallel", "parallel")),
        name="hgrn2",
    )(pa, pa, pa, pa, pa, gamma, norm_g.reshape(1, LANES))


def _softplus(z):
    return jnp.maximum(z, 0.0) + jnp.log(1.0 + jnp.exp(-jnp.abs(z)))


def _tri_inverse(n, r_i, c_i, mm):
    size = n.shape[0]
    blk = SUBLANES
    n0 = jnp.where(r_i // blk == c_i // blk, n, 0.0)
    t = jnp.where(r_i == c_i, 1.0, 0.0) + n0
    x = mm(n0, n0)
    t = t + mm(t, x)
    x = mm(x, x)
    t = t + mm(t, x)
    while blk < size:
        off = jnp.where((r_i // (2 * blk) == c_i // (2 * blk)) & (r_i // blk != c_i // blk), n, 0.0)
        t = t + mm(mm(t, off), t)
        blk *= 2
    return t


def _rwkv_body(r_ref, k_ref, v_ref, lo_ref, mur_ref, muk_ref, muv_ref, mulo_ref,
               w0_ref, w2_ref, a0_ref, a2_ref, g2_ref, kk_ref, ka_ref, rk_ref, lw_ref, lb_ref,
               o_ref,
               rs_ref, ks_ref, vs_ref, los_ref, yf_ref, yb_ref, bf_ref, bb_ref, h_ref, *, seq):
    c = RWKV_C
    n_chunks = seq // c

    def shift(src, mu_ref, dst):
        p = src[0]
        rows = lax.broadcasted_iota(jnp.int32, p.shape, 0)
        prev = jnp.where(rows == 0, 0.0, pltpu.roll(p, 1, 0))
        nxt = jnp.where(rows == seq - 1, 0.0, pltpu.roll(p, seq - 1, 0))
        dst[...] = p + mu_ref[0:1, :] * (prev - p) + mu_ref[1:2, :] * (nxt - p)

    shift(r_ref, mur_ref, rs_ref)
    shift(k_ref, muk_ref, ks_ref)
    shift(v_ref, muv_ref, vs_ref)
    shift(lo_ref, mulo_ref, los_ref)

    r_i = lax.broadcasted_iota(jnp.int32, (c, c), 0)
    c_i = lax.broadcasted_iota(jnp.int32, (c, c), 1)
    eye = r_i == c_i
    same_head = r_i // RWKV_DH == c_i // RWKV_DH
    head_sum = same_head.astype(F32)
    lane = lax.broadcasted_iota(jnp.int32, (1, LANES), 1)
    head_lanes = [lane < RWKV_DH, lane >= RWKV_DH]
    h_ref[...] = jnp.zeros_like(h_ref)

    def chunk(t0, d):
        incl = (c_i <= r_i) if d == 0 else (c_i >= r_i)
        strict = (c_i < r_i) if d == 0 else (c_i > r_i)
        tot_row = c - 1 if d == 0 else 0
        r = rs_ref[pl.ds(t0, c), :]
        k = ks_ref[pl.ds(t0, c), :]
        v = vs_ref[pl.ds(t0, c), :]
        lo = los_ref[pl.ds(t0, c), :]
        w = w0_ref[d:d + 1, :] + _mmf(jnp.tanh(lo[:, 0:LANES]), w2_ref[d])
        wlog = -jnp.exp(-_softplus(-w) - 0.5)
        a = _sigmoid(a0_ref[d:d + 1, :] + _mmf(lo[:, LANES:2 * LANES], a2_ref[d]))
        kk = k * kk_ref[...]
        kk = kk * lax.rsqrt(_mmf(kk * kk, head_sum) + 1e-12)
        kd = k * (1.0 + (a - 1.0) * ka_ref[...])
        bonus = _mmf(r * kd * rk_ref[...], head_sum) * v
        if d == 0:
            bf_ref[pl.ds(t0, c), :] = bonus
        else:
            bb_ref[pl.ds(t0, c), :] = bonus

        cum = _mmf(incl.astype(F32), wlog)
        mid = cum[c // 2:c // 2 + 1, :]
        tot = cum[tot_row:tot_row + 1, :]
        e1 = jnp.exp(cum - mid)
        e2 = jnp.exp(mid - cum)
        em = jnp.exp(mid)
        et = jnp.exp(tot - mid)
        a_rel = -kk * e1 * jnp.exp(-wlog)
        r_rel = r * e1
        b_inv = kk * a * e2
        k_inv = kd * e2
        a_dec = a_rel * em
        r_dec = r_rel * em
        b_end = b_inv * et
        k_end = k_inv * et

        w_parts, u_parts, q_parts, y_parts = [], [], [], []
        for hl in head_lanes:
            a_h = jnp.where(hl, a_rel, 0.0)
            r_h = jnp.where(hl, r_rel, 0.0)
            a_ab = jnp.where(strict, _mmb(a_h, b_inv, NT), 0.0)
            a_ak = jnp.where(strict, _mmb(a_h, k_inv, NT), 0.0)
            a_rb = jnp.where(incl, _mmb(r_h, b_inv, NT), 0.0)
            a_rk = jnp.where(incl, _mmb(r_h, k_inv, NT), 0.0)
            t = _tri_inverse(a_ab, r_i, c_i, _mmb)
            w_h = _mmb(t, a_dec)
            u_h = _mmb(t, _mmb(a_ak, v))
            w_parts.append(w_h)
            u_parts.append(u_h)
            q_parts.append(r_dec + _mmb(a_rb, w_h))
            y_parts.append(_mmb(a_rb, u_h) + _mmb(a_rk, v))
        pick = lambda parts: jnp.where(head_lanes[0], parts[0], parts[1])
        w_c, u_c, q_c, y_c = pick(w_parts), pick(u_parts), pick(q_parts), pick(y_parts)

        m = jnp.where(same_head, _mmb(b_end, w_c, TN), 0.0) + jnp.where(eye, jnp.exp(tot), 0.0)
        g = jnp.where(same_head, _mmb(b_end, u_c, TN) + _mmb(k_end, v, TN), 0.0)
        h0 = h_ref[d]
        y = _mmb(q_c, h0) + y_c
        h_ref[d] = _mmf(m, h0) + g
        if d == 0:
            yf_ref[pl.ds(t0, c), :] = y
        else:
            yb_ref[pl.ds(t0, c), :] = y

    def body(i, carry):
        chunk(pl.multiple_of(i * c, c), 0)
        chunk(pl.multiple_of((n_chunks - 1 - i) * c, c), 1)
        return carry

    lax.fori_loop(0, n_chunks, body, 0)

    inv_dh = 1.0 / RWKV_DH

    def epilogue(i, carry):
        t0 = pl.multiple_of(i * c, c)
        y = yf_ref[pl.ds(t0, c), :] + yb_ref[pl.ds(t0, c), :]
        yc = y - _mmf(y, head_sum) * inv_dh
        var = _mmf(yc * yc, head_sum) * inv_dh
        yn = yc * lax.rsqrt(var + RWKV_GN_EPS) * lw_ref[...] + lb_ref[...]
        gate = _mmf(_sigmoid(los_ref[pl.ds(t0, c), 2 * LANES:3 * LANES]), g2_ref[...])
        o_ref[0, pl.ds(t0, c), :] = (yn + bf_ref[pl.ds(t0, c), :] + bb_ref[pl.ds(t0, c), :]) * gate
        return carry

    lax.fori_loop(0, n_chunks, epilogue, 0)


def _rwkv(pb, mu, w0, w2, a0, a2, g2, k_k, k_a, r_k, lnx_w, lnx_b):
    b, s, n_in = pb.shape
    width = w0.shape[-1]
    pairs = width // LANES
    lo_w = n_in - 3 * width
    assert lo_w == 3 * LANES and (3 * width) % lo_w == 0
    lo_blk = 3 * width // lo_w
    lora = w2.shape[1]
    zero = jnp.zeros((lora, width), F32)
    w2p = jnp.stack([jnp.concatenate([w2[0], zero]), jnp.concatenate([zero, w2[1]])])
    a2p = jnp.stack([jnp.concatenate([a2[0], zero]), jnp.concatenate([zero, a2[1]])])
    colv = lambda k: pl.BlockSpec((1, s, LANES), lambda i, j, k=k: (i, 0, k * pairs + j))
    mucol = lambda k: pl.BlockSpec((2, LANES), lambda i, j, k=k: (0, k * pairs + j))
    vec = pl.BlockSpec((1, LANES), lambda i, j: (0, j))
    vec2 = pl.BlockSpec((2, LANES), lambda i, j: (0, j))
    fac = pl.BlockSpec((2, LANES, LANES), lambda i, j: (0, 0, j))
    seqbuf = pltpu.VMEM((s, LANES), F32)
    return pl.pallas_call(
        functools.partial(_rwkv_body, seq=s),
        grid=(b, pairs),
        in_specs=[colv(0), colv(1), colv(2),
                  pl.BlockSpec((1, s, lo_w), lambda i, j: (i, 0, lo_blk)),
                  mucol(0), mucol(1), mucol(2),
                  pl.BlockSpec((2, lo_w), lambda i, j: (0, lo_blk)),
                  vec2, fac, vec2, fac,
                  pl.BlockSpec((LANES, LANES), lambda i, j: (0, j)),
                  vec, vec, vec, vec, vec],
        out_specs=pl.BlockSpec((1, s, LANES), lambda i, j: (i, 0, j)),
        out_shape=jax.ShapeDtypeStruct((b, s, width), F32),
        scratch_shapes=[seqbuf, seqbuf, seqbuf, pltpu.VMEM((s, lo_w), F32),
                        seqbuf, seqbuf, seqbuf, seqbuf,
                        pltpu.VMEM((2, LANES, LANES), F32)],
        compiler_params=_cparams(("parallel", "parallel")),
        name="rwkv7",
    )(pb, pb, pb, pb, mu, mu, mu, mu, w0, w2p, a0, a2p, g2,
      k_k.reshape(1, width), k_a.reshape(1, width), r_k.reshape(1, width),
      lnx_w.reshape(1, width), lnx_b.reshape(1, width))


def _fnet_body(u_ref, cs_ref, f_ref, o_ref, z_ref, *, seq, groups, tr):
    for g in range(groups):
        z = _mmb(u_ref[0, :, g * LANES:(g + 1) * LANES], cs_ref[...])
        z_ref[0:seq, g * LANES:(g + 1) * LANES] = z[:, :LANES].astype(BF16)
        z_ref[seq:2 * seq, g * LANES:(g + 1) * LANES] = z[:, LANES:].astype(BF16)
    scale = float(1.0 / np.sqrt(float(seq * LANES)))

    def rows(i, carry):
        t0 = pl.multiple_of(i * tr, tr)
        o_ref[0, pl.ds(t0, tr), :] = jnp.dot(f_ref[pl.ds(t0, tr), :], z_ref[...],
                                             preferred_element_type=F32) * scale
        return carry

    lax.fori_loop(0, seq // tr, rows, 0)


def _dft_tables(seq):
    def table(n):
        idx = jnp.arange(n, dtype=jnp.int32)
        ang = ((idx[:, None] * idx[None, :]) % n).astype(F32) * float(2.0 * np.pi / n)
        return jnp.cos(ang), jnp.sin(ang)
    cs, ss = table(seq)
    cc, sc = table(LANES)
    return (jnp.concatenate([cs, -ss], axis=1).astype(BF16),
            jnp.concatenate([cc, sc], axis=1).astype(BF16))


def _fnet(pc):
    b, s, w = pc.shape
    groups = w // LANES
    f_tab, c_tab = _dft_tables(s)
    tr = min(s, 512)
    return pl.pallas_call(
        functools.partial(_fnet_body, seq=s, groups=groups, tr=tr),
        grid=(b,),
        in_specs=[pl.BlockSpec((1, s, w), lambda i: (i, 0, 0)),
                  _const_spec((LANES, 2 * LANES)),
                  _const_spec((s, 2 * s))],
        out_specs=pl.BlockSpec((1, s, w), lambda i: (i, 0, 0)),
        out_shape=jax.ShapeDtypeStruct((b, s, w), F32),
        scratch_shapes=[pltpu.VMEM((2 * s, w), BF16)],
        compiler_params=_cparams(("parallel",)),
        name="fnet",
    )(pc, c_tab, f_tab)


def _conformer_body(u_ref, w_ref, b_ref, g_ref, beta_ref, o_ref, hp_ref, *, seq, width, tr):
    zeros = jnp.zeros((CONV_PAD, width), F32)
    hp_ref[0:CONV_PAD, :] = zeros
    hp_ref[CONV_PAD + seq:CONV_PAD + seq + CONV_PAD, :] = zeros

    def glu(i, carry):
        t0 = pl.multiple_of(i * tr, tr)
        val = u_ref[0, pl.ds(t0, tr), 0:width]
        gate = u_ref[0, pl.ds(t0, tr), width:2 * width]
        hp_ref[pl.ds(CONV_PAD + t0, tr), :] = val * _sigmoid(gate)
        return carry

    lax.fori_loop(0, seq // tr, glu, 0)

    def conv(i, carry):
        t0 = pl.multiple_of(i * tr, tr)
        win = hp_ref[pl.ds(t0, tr + 2 * CONV_PAD), :]
        acc = jnp.zeros((tr, width), F32) + b_ref[...]
        for sub in range(SUBLANES):
            rot = win if sub == 0 else pltpu.roll(win, tr + 2 * CONV_PAD - sub, 0)
            for j in range(CONV_K):
                off = j + CONV_PAD - CONV_K // 2
                if off % SUBLANES == sub:
                    base = off - sub
                    acc = acc + w_ref[j:j + 1, :] * rot[base:base + tr]
        mu = jnp.mean(acc, axis=-1, keepdims=True)
        xc = acc - mu
        y = xc * lax.rsqrt(jnp.mean(xc * xc, axis=-1, keepdims=True) + LN_EPS)
        o_ref[0, pl.ds(t0, tr), :] = _silu(y * g_ref[...] + beta_ref[...])
        return carry

    lax.fori_loop(0, seq // tr, conv, 0)


def _conformer(pd, conv_w, conv_b, ln_g, ln_b):
    b, s, w2 = pd.shape
    w = w2 // 2
    tr = 64
    vec = pl.BlockSpec((1, w), lambda i: (0, 0))
    return pl.pallas_call(
        functools.partial(_conformer_body, seq=s, width=w, tr=tr),
        grid=(b,),
        in_specs=[pl.BlockSpec((1, s, w2), lambda i: (i, 0, 0)),
                  pl.BlockSpec((CONV_K, w), lambda i: (0, 0)),
                  vec, vec, vec],
        out_specs=pl.BlockSpec((1, s, w), lambda i: (i, 0, 0)),
        out_shape=jax.ShapeDtypeStruct((b, s, w), F32),
        scratch_shapes=[pltpu.VMEM((s + 2 * CONV_PAD, w), F32)],
        compiler_params=_cparams(("parallel",)),
        name="conformer_conv",
    )(pd, conv_w, conv_b.reshape(1, w), ln_g.reshape(1, w), ln_b.reshape(1, w))


def _outproj_body(x_ref, ya_ref, yb_ref, gate_ref, w_ref, o_ref):
    mix = (jnp.dot(ya_ref[0].astype(BF16), w_ref[0], preferred_element_type=F32)
           + jnp.dot(yb_ref[0].astype(BF16), w_ref[1], preferred_element_type=F32))
    o_ref[0] = x_ref[0] + gate_ref[0] * mix


def _outproj(x, ya, yb, gate, w_out, tm):
    b, s, d = x.shape
    half = ya.shape[-1]
    tile = lambda n: pl.BlockSpec((1, tm, n), lambda i, j: (i, j, 0))
    return pl.pallas_call(
        _outproj_body,
        grid=(b, s // tm),
        in_specs=[tile(d), tile(half), tile(half),
                  pl.BlockSpec((1, 1, d), lambda i, j: (i, 0, 0)),
                  _const_spec((2, half, d))],
        out_specs=tile(d),
        out_shape=jax.ShapeDtypeStruct((b, s, d), F32),
        compiler_params=_cparams(("parallel", "parallel")),
        name="outproj",
    )(x, ya, yb, gate, w_out.reshape(2, half, d).astype(BF16))


def _ffn_body(x_ref, xp_ref, xn_ref, g_ref, sc_ref, sh_ref, gate_ref, wu_ref, wv_ref, cw_ref, cb_ref,
              wd_ref, fg_ref, o_ref, acc_ref, *, seq, tm, final_norm):
    rows = tm + 2 * HALO
    xa = jnp.concatenate([xp_ref[0], x_ref[0], xn_ref[0]], axis=0)
    hb = _modulated_norm(xa, g_ref[...], sc_ref[0], sh_ref[0]).astype(BF16)
    pos = pl.program_id(1) * tm - HALO + lax.broadcasted_iota(jnp.int32, (rows, 1), 0)
    inside = (pos >= 0) & (pos < seq)
    acc_ref[...] = jnp.zeros_like(acc_ref)

    def step(ci, carry):
        u = jnp.dot(hb, wu_ref[ci], preferred_element_type=F32)
        u = jnp.where(inside, u, 0.0)
        v = jnp.dot(hb[HALO:HALO + tm], wv_ref[ci], preferred_element_type=F32)
        cw = cw_ref[ci]
        u_prev = pltpu.roll(u, 1, 0)[HALO:HALO + tm]
        u_next = pltpu.roll(u, rows - 1, 0)[HALO:HALO + tm]
        uc = cw[0:1, :] * u_prev + cw[1:2, :] * u[HALO:HALO + tm] + cw[2:3, :] * u_next + cb_ref[ci]
        act = (_silu(uc) * v).astype(BF16)
        acc_ref[...] += jnp.dot(act, wd_ref[ci], preferred_element_type=F32)
        return carry

    lax.fori_loop(0, wu_ref.shape[0], step, 0)
    y = x_ref[0] + gate_ref[0] * acc_ref[...]
    if final_norm:
        ms = jnp.mean(y * y, axis=-1, keepdims=True)
        y = y * lax.rsqrt(ms + RMS_EPS) * fg_ref[...]
    o_ref[0] = y


def _ffn(x, g, scale, shift, gate, w_up, conv_w, conv_b, w_down, final_g, final_norm, tm):
    b, s, d = x.shape
    dff = w_down.shape[0]
    assert dff % FF_CHUNK == 0 and s % tm == 0 and tm % HALO == 0
    nch = dff // FF_CHUNK
    chunks = lambda w: w.reshape(d, nch, FF_CHUNK).transpose(1, 0, 2).astype(BF16)
    wu, wv = chunks(w_up[:, :dff]), chunks(w_up[:, dff:])
    cw = conv_w.reshape(conv_w.shape[0], nch, FF_CHUNK).transpose(1, 0, 2)
    cb = conv_b.reshape(nch, 1, FF_CHUNK)
    wd = w_down.reshape(nch, FF_CHUNK, d).astype(BF16)
    hb = tm // HALO
    last = s // HALO - 1
    row = pl.BlockSpec((1, 1, d), lambda i, j: (i, 0, 0))
    vec = pl.BlockSpec((1, d), lambda i, j: (0, 0))
    return pl.pallas_call(
        functools.partial(_ffn_body, seq=s, tm=tm, final_norm=final_norm),
        grid=(b, s // tm),
        in_specs=[pl.BlockSpec((1, tm, d), lambda i, j: (i, j, 0)),
                  pl.BlockSpec((1, HALO, d), lambda i, j: (i, jnp.maximum(j * hb - 1, 0), 0)),
                  pl.BlockSpec((1, HALO, d), lambda i, j: (i, jnp.minimum((j + 1) * hb, last), 0)),
                  vec, row, row, row,
                  _const_spec((nch, d, FF_CHUNK)), _const_spec((nch, d, FF_CHUNK)),
                  _const_spec((nch, conv_w.shape[0], FF_CHUNK)), _const_spec((nch, 1, FF_CHUNK)),
                  _const_spec((nch, FF_CHUNK, d)),
                  vec],
        out_specs=pl.BlockSpec((1, tm, d), lambda i, j: (i, j, 0)),
        out_shape=jax.ShapeDtypeStruct((b, s, d), F32),
        scratch_shapes=[pltpu.VMEM((tm, d), F32)],
        compiler_params=_cparams(("parallel", "parallel")),
        name="conv_ffn",
    )(x, x, x, g.reshape(1, d), scale, shift, gate, wu, wv, cw, cb, wd, final_g.reshape(1, d))


def kernel(x, c, ada_w, ada_b, norm_g, final_g, ab_w_in, ab_w_out, hgrn_gamma, hgrn_norm_g, rwkv_mu, rwkv_w0, rwkv_w2, rwkv_a0, rwkv_a2, rwkv_g2, rwkv_kk, rwkv_ka, rwkv_rk, rwkv_lnx_w, rwkv_lnx_b, cd_w_in, cd_w_out, dconv_w, dconv_b, dconv_ln_g, dconv_ln_b, ffn_w_up, ffn_conv_w, ffn_conv_b, ffn_w_down):
    bsz, seq, d = x.shape
    depth = ada_w.shape[0]
    a_in = 5 * (hgrn_gamma.shape[-1])
    c_width = dconv_w.shape[-1]
    tm = min(seq, 512)

    mod = _adaln(c, ada_w.reshape(depth * 2, d, 3 * d), ada_b.reshape(depth * 2, 3 * d))
    mod = mod.reshape(depth, 2, bsz, 3, 1, d)

    for l in range(depth):
        j = l // 2
        shift, scale, gate = mod[l, 0, :, 0], mod[l, 0, :, 1], mod[l, 0, :, 2]
        if l % 2 == 0:
            pa, pb = _inproj(x, norm_g[l, 0], scale, shift, ab_w_in[j],
                             (a_in, ab_w_in.shape[-1] - a_in), tm)
            y1 = _hgrn(pa, hgrn_gamma, hgrn_norm_g[j], l)
            y2 = _rwkv(pb, rwkv_mu[j], rwkv_w0[j], rwkv_w2[j], rwkv_a0[j], rwkv_a2[j], rwkv_g2[j],
                       rwkv_kk[j], rwkv_ka[j], rwkv_rk[j], rwkv_lnx_w[j], rwkv_lnx_b[j])
            w_out = ab_w_out[j]
        else:
            pc, pd = _inproj(x, norm_g[l, 0], scale, shift, cd_w_in[j],
                             (cd_w_in.shape[-1] - 2 * c_width, 2 * c_width), tm)
            y1 = _fnet(pc)
            y2 = _conformer(pd, dconv_w[j], dconv_b[j], dconv_ln_g[j], dconv_ln_b[j])
            w_out = cd_w_out[j]
        x = _outproj(x, y1, y2, gate, w_out, tm)
        shift, scale, gate = mod[l, 1, :, 0], mod[l, 1, :, 1], mod[l, 1, :, 2]
        x = _ffn(x, norm_g[l, 1], scale, shift, gate, ffn_w_up[l], ffn_conv_w[l], ffn_conv_b[l],
                 ffn_w_down[l], final_g, l == depth - 1, tm)
    return x
```

```python
import functools

import numpy as np
import jax
import jax.numpy as jnp
from jax import lax
from jax.experimental import pallas as pl
from jax.experimental.pallas import tpu as pltpu

F32 = jnp.float32
BF16 = jnp.bfloat16
HI = lax.Precision.HIGHEST

LANES = 128
SUBLANES = 8
VMEM_LIMIT_BYTES = 56 * 1024 * 1024

RMS_EPS = 1e-6
LN_EPS = 1e-5
RWKV_GN_EPS = 64e-5
RWKV_DH = 64
CONV_K = 31
CONV_PAD = 16
HGRN_C = 64
HGRN_NB = 4
RWKV_C = 128
RWKV_NB = 2
FF_CHUNK = 256
HALO = 2 * SUBLANES

NN = (((1,), (0,)), ((), ()))
NT = (((1,), (1,)), ((), ()))
TN = (((0,), (0,)), ((), ()))


def _mmf(a, b, dims=NN):
    return lax.dot_general(a, b, dims, precision=HI, preferred_element_type=F32)


def _mmb(a, b, dims=NN):
    return lax.dot_general(a.astype(BF16), b.astype(BF16), dims, preferred_element_type=F32)


def _sigmoid(x):
    return jax.nn.sigmoid(x)


def _silu(x):
    return x * jax.nn.sigmoid(x)


def _cparams(sem):
    return pltpu.CompilerParams(dimension_semantics=sem, vmem_limit_bytes=VMEM_LIMIT_BYTES)


def _const_spec(shape):
    nd = len(shape)
    return pl.BlockSpec(shape, lambda *_: (0,) * nd, pipeline_mode=pl.Buffered(1))


def _adaln_body(c_ref, w_ref, b_ref, o_ref):
    o_ref[0] = _mmf(_silu(c_ref[...]), w_ref[0]) + b_ref[0]


def _adaln(c, ada_w, ada_b):
    k, d, d3 = ada_w.shape
    b = c.shape[0]
    return pl.pallas_call(
        _adaln_body,
        grid=(k, d3 // d),
        in_specs=[pl.BlockSpec((b, d), lambda i, j: (0, 0)),
                  pl.BlockSpec((1, d, d), lambda i, j: (i, 0, j)),
                  pl.BlockSpec((1, 1, d), lambda i, j: (i, 0, j))],
        out_specs=pl.BlockSpec((1, b, d), lambda i, j: (i, 0, j)),
        out_shape=jax.ShapeDtypeStruct((k, b, d3), F32),
        compiler_params=_cparams(("parallel", "parallel")),
        name="adaln",
    )(c, ada_w, ada_b.reshape(k, 1, d3))


def _modulated_norm(x, g, scale, shift):
    ms = jnp.mean(x * x, axis=-1, keepdims=True)
    return (x * lax.rsqrt(ms + RMS_EPS) * g) * (1.0 + scale) + shift


def _inproj_body(x_ref, g_ref, sc_ref, sh_ref, w_ref, *o_refs):
    hb = _modulated_norm(x_ref[0], g_ref[...], sc_ref[0], sh_ref[0]).astype(BF16)
    off = 0
    for o_ref in o_refs:
        n = o_ref.shape[-1]
        o_ref[0] = jnp.dot(hb, w_ref[:, off:off + n], preferred_element_type=F32).astype(o_ref.dtype)
        off += n


def _inproj(x, g, scale, shift, w, splits, tm):
    b, s, d = x.shape
    n = w.shape[1]
    assert sum(splits) == n and s % tm == 0
    row = pl.BlockSpec((1, 1, d), lambda i, j: (i, 0, 0))
    return pl.pallas_call(
        _inproj_body,
        grid=(b, s // tm),
        in_specs=[pl.BlockSpec((1, tm, d), lambda i, j: (i, j, 0)),
                  pl.BlockSpec((1, d), lambda i, j: (0, 0)),
                  row, row,
                  _const_spec((d, n))],
        out_specs=[pl.BlockSpec((1, tm, m), lambda i, j: (i, j, 0)) for m in splits],
        out_shape=[jax.ShapeDtypeStruct((b, s, m), F32) for m in splits],
        compiler_params=_cparams(("parallel", "parallel")),
        name="inproj",
    )(x, g.reshape(1, d), scale, shift, w.astype(BF16))


def _split2(x):
    hi = x.astype(BF16)
    return hi, (x - hi.astype(F32)).astype(BF16)


def _split3(x):
    hi = x.astype(BF16)
    rem = x - hi.astype(F32)
    mid = rem.astype(BF16)
    return hi, mid, (rem - mid.astype(F32)).astype(BF16)


def _mask_mm3(mask_bf16, x):
    n = x.shape[1]
    out = jnp.dot(mask_bf16, jnp.concatenate(_split3(x), axis=1), preferred_element_type=F32)
    return out[:, :n] + out[:, n:2 * n] + out[:, 2 * n:]


def _mm_mask2(x, mask_bf16):
    m = x.shape[0]
    out = jnp.dot(jnp.concatenate(_split2(x), axis=0), mask_bf16, preferred_element_type=F32)
    return out[:m] + out[m:]


def _hgrn_body(q_ref, ff_ref, fb_ref, i_ref, g_ref, gam_ref, ng_ref, o_ref,
               of_ref, ob_ref, st_ref, *, seq, layer):
    c = HGRN_C
    nb = HGRN_NB
    n_chunks = seq // c
    rows = [gam_ref[pl.ds(i, 1), :] for i in range(gam_ref.shape[0])]
    mx = functools.reduce(jnp.maximum, rows)
    es = [jnp.exp(r - mx) for r in rows]
    lb = sum(es[:layer + 1]) / sum(es)

    r_i = lax.broadcasted_iota(jnp.int32, (c, c), 0)
    c_i = lax.broadcasted_iota(jnp.int32, (c, c), 1)
    masks = [c_i <= r_i, c_i >= r_i]
    masks_bf = [jnp.where(m, 1.0, 0.0).astype(BF16) for m in masks]
    tot_rows = [c - 1, 0]
    fraw_refs = [ff_ref, fb_ref]
    out_refs = [of_ref, ob_ref]
    st_ref[...] = jnp.zeros_like(st_ref)

    def body(i, carry):
        probs = []
        for d in range(2):
            for u in range(nb):
                ci = i * nb + u
                probs.append((d, pl.multiple_of((ci if d == 0 else n_chunks - 1 - ci) * c, c)))
        q = [q_ref[0, pl.ds(t0, c), :] for d, t0 in probs]
        v = [i_ref[0, pl.ds(t0, c), :] for d, t0 in probs]
        f = [lb + (1.0 - lb) * _sigmoid(fraw_refs[d][0, pl.ds(t0, c), :]) for d, t0 in probs]
        cum = [_mask_mm3(masks_bf[d], jnp.log(fj)) for (d, t0), fj in zip(probs, f)]
        k = [1.0 - fj for fj in f]
        mid = [cj[c // 2:c // 2 + 1, :] for cj in cum]
        tot = [cj[tot_rows[d]:tot_rows[d] + 1, :] for (d, t0), cj in zip(probs, cum)]
        scores = [_mmb(qj * jnp.exp(cj - mj), kj * jnp.exp(mj - cj), NT)
                  for qj, kj, cj, mj in zip(q, k, cum, mid)]
        scores = [jnp.where(masks[d], sj, 0.0) for (d, t0), sj in zip(probs, scores)]
        intra = [_mmb(sj, vj) for sj, vj in zip(scores, v)]
        kv = [_mmb(vj, kj * jnp.exp(tj - cj), TN) for vj, kj, tj, cj in zip(v, k, tot, cum)]
        q_dec = [qj * jnp.exp(cj) for qj, cj in zip(q, cum)]
        dec = [jnp.exp(tj) for tj in tot]
        for d in range(2):
            st = st_ref[d]
            for j, (dj, t0) in enumerate(probs):
                if dj == d:
                    out_refs[d][pl.ds(t0, c), :] = intra[j] + _mmb(q_dec[j], st, NT)
                    st = st * dec[j] + kv[j]
            st_ref[d] = st
        return carry

    lax.fori_loop(0, n_chunks // nb, body, 0)

    def epilogue(i, carry):
        t0 = pl.multiple_of(i * c, c)
        o = of_ref[pl.ds(t0, c), :] + ob_ref[pl.ds(t0, c), :]
        ms = jnp.mean(o * o, axis=-1, keepdims=True)
        y = o * lax.rsqrt(ms + RMS_EPS) * ng_ref[...]
        o_ref[0, pl.ds(t0, c), :] = y * _silu(g_ref[0, pl.ds(t0, c), :])
        return carry

    lax.fori_loop(0, n_chunks, epilogue, 0)


def _hgrn(pa, gamma, norm_g, layer):
    b, s, w5 = pa.shape
    heads = w5 // 5 // LANES
    assert s % (HGRN_C * HGRN_NB) == 0
    col = lambda k: pl.BlockSpec((1, s, LANES), lambda i, h, k=k: (i, 0, k * heads + h))
    return pl.pallas_call(
        functools.partial(_hgrn_body, seq=s, layer=layer),
        grid=(b, heads),
        in_specs=[col(0), col(1), col(2), col(3), col(4),
                  pl.BlockSpec((gamma.shape[0], LANES), lambda i, h: (0, h)),
                  pl.BlockSpec((1, LANES), lambda i, h: (0, 0))],
        out_specs=pl.BlockSpec((1, s, LANES), lambda i, h: (i, 0, h)),
        out_shape=jax.ShapeDtypeStruct((b, s, heads * LANES), F32),
        scratch_shapes=[pltpu.VMEM((s, LANES), F32), pltpu.VMEM((s, LANES), F32),
                        pltpu.VMEM((2, LANES, LANES), F32)],
        compiler_params=_cparams(("parallel", "parallel")),
        name="hgrn2",
    )(pa, pa, pa, pa, pa, gamma, norm_g.reshape(1, LANES))


def _softplus(z):
    return jnp.maximum(z, 0.0) + jnp.log(1.0 + jnp.exp(-jnp.abs(z)))


def _tri_inverses(ns, r_i, c_i):
    size = ns[0].shape[0]
    blk = SUBLANES
    diag_blocks = r_i // blk == c_i // blk
    eye = jnp.where(r_i == c_i, 1.0, 0.0)
    n0 = [jnp.where(diag_blocks, n, 0.0).astype(BF16) for n in ns]
    t = [eye + n for n in n0]
    x = [jnp.dot(n, n, preferred_element_type=F32).astype(BF16) for n in n0]
    t = [tj + _mmb(tj, xj) for tj, xj in zip(t, x)]
    x = [jnp.dot(xj, xj, preferred_element_type=F32).astype(BF16) for xj in x]
    t = [tj + _mmb(tj, xj) for tj, xj in zip(t, x)]
    while blk < size:
        pair = (r_i // (2 * blk) == c_i // (2 * blk)) & (r_i // blk != c_i // blk)
        tb = [tj.astype(BF16) for tj in t]
        p = [jnp.dot(tj, jnp.where(pair, n, 0.0).astype(BF16), preferred_element_type=F32).astype(BF16)
             for tj, n in zip(tb, ns)]
        t = [tj + jnp.dot(pj, tbj, preferred_element_type=F32) for tj, pj, tbj in zip(t, p, tb)]
        blk *= 2
    return t


def _rwkv_body(r_ref, k_ref, v_ref, lo_ref, mur_ref, muk_ref, muv_ref, mulo_ref,
               w0_ref, w2_ref, a0_ref, a2_ref, g2_ref, kk_ref, ka_ref, rk_ref, lw_ref, lb_ref,
               o_ref,
               rs_ref, ks_ref, vs_ref, los_ref, qp_ref, y0_ref, m_ref, gm_ref, bon_ref,
               yf_ref, yb_ref, h_ref, *, seq):
    c = RWKV_C
    nb = RWKV_NB
    n_chunks = seq // c

    def shift(src, mu_ref, dst):
        p = src[0]
        rows = lax.broadcasted_iota(jnp.int32, p.shape, 0)
        prev = jnp.where(rows == 0, 0.0, pltpu.roll(p, 1, 0))
        nxt = jnp.where(rows == seq - 1, 0.0, pltpu.roll(p, seq - 1, 0))
        dst[...] = p + mu_ref[0:1, :] * (prev - p) + mu_ref[1:2, :] * (nxt - p)

    shift(r_ref, mur_ref, rs_ref)
    shift(k_ref, muk_ref, ks_ref)
    shift(v_ref, muv_ref, vs_ref)
    shift(lo_ref, mulo_ref, los_ref)

    r_i = lax.broadcasted_iota(jnp.int32, (c, c), 0)
    c_i = lax.broadcasted_iota(jnp.int32, (c, c), 1)
    eye = r_i == c_i
    same_head = r_i // RWKV_DH == c_i // RWKV_DH
    head_sum = jnp.where(same_head, 1.0, 0.0).astype(BF16)
    incl = [c_i <= r_i, c_i >= r_i]
    incl_bf = [jnp.where(m, 1.0, 0.0).astype(BF16) for m in incl]
    r_q = lax.broadcasted_iota(jnp.int32, (2 * c, 2 * c), 0)
    c_q = lax.broadcasted_iota(jnp.int32, (2 * c, 2 * c), 1)
    r_l, c_l = r_q % c, c_q % c
    diag_r = (r_q >= c) & (r_l == c_l)
    quad = [(c_l < r_l) | diag_r, (c_l > r_l) | diag_r]
    tot_rows = [c - 1, 0]
    lane = lax.broadcasted_iota(jnp.int32, (1, LANES), 1)
    head_lanes = [lane < RWKV_DH, lane >= RWKV_DH]
    lane2 = lax.broadcasted_iota(jnp.int32, (1, 2 * LANES), 1)
    first_head2 = (lane2 % LANES) < RWKV_DH
    zeros = jnp.zeros((c, LANES), F32)

    def phase_a(i, carry):
        probs = []
        for u in range(nb):
            ci = i * nb + u
            t0 = pl.multiple_of(ci * c, c)
            r = rs_ref[pl.ds(t0, c), :]
            k = ks_ref[pl.ds(t0, c), :]
            v = vs_ref[pl.ds(t0, c), :]
            lo = los_ref[pl.ds(t0, c), :]
            kk = k * kk_ref[...]
            kk = kk * lax.rsqrt(_mm_mask2(kk * kk, head_sum) + 1e-12)
            tanh_w = jnp.tanh(lo[:, 0:LANES])
            bonus = zeros
            for d in range(2):
                w = w0_ref[d:d + 1, :] + _mmb(tanh_w, w2_ref[d])
                wlog = -jnp.exp(-_softplus(-w) - 0.5)
                a = _sigmoid(a0_ref[d:d + 1, :] + _mmb(lo[:, LANES:2 * LANES], a2_ref[d]))
                kd = k * (1.0 + (a - 1.0) * ka_ref[...])
                bonus = bonus + _mm_mask2(r * kd * rk_ref[...], head_sum) * v
                cum = _mask_mm3(incl_bf[d], wlog)
                mid = cum[c // 2:c // 2 + 1, :]
                tot = cum[tot_rows[d]:tot_rows[d] + 1, :]
                e1 = jnp.exp(cum - mid)
                e2 = jnp.exp(mid - cum)
                em = jnp.exp(mid)
                et = jnp.exp(tot - mid)
                a_rel = -kk * e1 * jnp.exp(-wlog)
                r_rel = r * e1
                b_inv = kk * a * e2
                k_inv = kd * e2
                probs.append(dict(ci=ci, t0=t0, d=d, v=v, a_rel=a_rel, r_rel=r_rel,
                                  a_dec=a_rel * em, r_dec=r_rel * em,
                                  rhs=jnp.concatenate([b_inv, k_inv], axis=0).astype(BF16),
                                  ends=jnp.concatenate([b_inv * et, k_inv * et], axis=0).astype(BF16),
                                  gamma=jnp.exp(tot)))
            bon_ref[pl.ds(t0, c), :] = bonus

        chains = [(p, hl) for p in probs for hl in head_lanes]
        lhs = [jnp.concatenate([jnp.where(hl, p["a_rel"], 0.0), jnp.where(hl, p["r_rel"], 0.0)],
                               axis=0).astype(BF16) for p, hl in chains]
        blocks = [lax.dot_general(lj, p["rhs"], NT, preferred_element_type=F32)
                  for lj, (p, hl) in zip(lhs, chains)]
        blocks = [jnp.where(quad[p["d"]], bj, 0.0) for bj, (p, hl) in zip(blocks, chains)]
        a_ab = [bj[:c, :c] for bj in blocks]
        a_ak = [bj[:c, c:].astype(BF16) for bj in blocks]
        a_r = [bj[c:, :].astype(BF16) for bj in blocks]
        vb = [p["v"].astype(BF16) for p, hl in chains]
        av = [jnp.dot(aj, vj, preferred_element_type=F32) for aj, vj in zip(a_ak, vb)]
        t = _tri_inverses(a_ab, r_i, c_i)
        wu = [_mmb(tj, jnp.concatenate([p["a_dec"], avj], axis=1))
              for tj, avj, (p, hl) in zip(t, av, chains)]
        lower = [jnp.concatenate([zeros, p["v"]], axis=1).astype(BF16) for p in probs]
        qy = [jnp.dot(arj, jnp.concatenate([wuj.astype(BF16), lower[j // 2]], axis=0),
                      preferred_element_type=F32) for j, (arj, wuj) in enumerate(zip(a_r, wu))]
        for j, p in enumerate(probs):
            d, ci, t0 = p["d"], p["ci"], p["t0"]
            wu_p = jnp.where(first_head2, wu[2 * j], wu[2 * j + 1])
            qy_p = jnp.where(first_head2, qy[2 * j], qy[2 * j + 1])
            qp_ref[d, pl.ds(t0, c), :] = (p["r_dec"] + qy_p[:, :c]).astype(BF16)
            y0_ref[d, pl.ds(t0, c), :] = qy_p[:, c:]
            mg = lax.dot_general(p["ends"], jnp.concatenate([wu_p.astype(BF16), lower[j]], axis=0),
                                 TN, preferred_element_type=F32)
            m = jnp.where(same_head, mg[:, :c], 0.0) + jnp.where(eye, p["gamma"], 0.0)
            m_hi, m_lo = _split2(m)
            m_ref[d, ci] = jnp.concatenate([m_hi, m_hi, m_lo], axis=1)
            gm_ref[d, ci] = jnp.where(same_head, mg[:, c:], 0.0)
        return carry

    lax.fori_loop(0, n_chunks // nb, phase_a, 0)

    h_ref[...] = jnp.zeros_like(h_ref)

    def phase_b(i, carry):
        cis = [i, n_chunks - 1 - i]
        t0s = [pl.multiple_of(ci * c, c) for ci in cis]
        h0 = [h_ref[d] for d in range(2)]
        hs = [_split2(h) for h in h0]
        y = [jnp.dot(qp_ref[d, pl.ds(t0s[d], c), :], hs[d][0], preferred_element_type=F32)
             + y0_ref[d, pl.ds(t0s[d], c), :] for d in range(2)]
        hn = [jnp.dot(m_ref[d, cis[d]], jnp.concatenate([hs[d][0], hs[d][1], hs[d][0]], axis=0),
                      preferred_element_type=F32) + gm_ref[d, cis[d]] for d in range(2)]
        yf_ref[pl.ds(t0s[0], c), :] = y[0]
        yb_ref[pl.ds(t0s[1], c), :] = y[1]
        for d in range(2):
            h_ref[d] = hn[d]
        return carry

    lax.fori_loop(0, n_chunks, phase_b, 0)

    inv_dh = 1.0 / RWKV_DH

    def phase_c(i, carry):
        t0 = pl.multiple_of(i * c, c)
        y = yf_ref[pl.ds(t0, c), :] + yb_ref[pl.ds(t0, c), :]
        yc = y - _mm_mask2(y, head_sum) * inv_dh
        var = _mm_mask2(yc * yc, head_sum) * inv_dh
        yn = yc * lax.rsqrt(var + RWKV_GN_EPS) * lw_ref[...] + lb_ref[...]
        gate = _mmb(_sigmoid(los_ref[pl.ds(t0, c), 2 * LANES:3 * LANES]), g2_ref[...])
        o_ref[0, pl.ds(t0, c), :] = (yn + bon_ref[pl.ds(t0, c), :]) * gate
        return carry

    lax.fori_loop(0, n_chunks, phase_c, 0)


def _rwkv(pb, mu, w0, w2, a0, a2, g2, k_k, k_a, r_k, lnx_w, lnx_b):
    b, s, n_in = pb.shape
    width = w0.shape[-1]
    pairs = width // LANES
    lo_w = n_in - 3 * width
    assert lo_w == 3 * LANES and (3 * width) % lo_w == 0 and s % (RWKV_C * RWKV_NB) == 0
    lo_blk = 3 * width // lo_w
    lora = w2.shape[1]
    n_chunks = s // RWKV_C
    zero = jnp.zeros((lora, width), F32)
    w2p = jnp.stack([jnp.concatenate([w2[0], zero]), jnp.concatenate([zero, w2[1]])])
    a2p = jnp.stack([jnp.concatenate([a2[0], zero]), jnp.concatenate([zero, a2[1]])])
    colv = lambda k: pl.BlockSpec((1, s, LANES), lambda i, j, k=k: (i, 0, k * pairs + j))
    mucol = lambda k: pl.BlockSpec((2, LANES), lambda i, j, k=k: (0, k * pairs + j))
    vec = pl.BlockSpec((1, LANES), lambda i, j: (0, j))
    vec2 = pl.BlockSpec((2, LANES), lambda i, j: (0, j))
    fac = pl.BlockSpec((2, LANES, LANES), lambda i, j: (0, 0, j))
    seqbuf = pltpu.VMEM((s, LANES), F32)
    return pl.pallas_call(
        functools.partial(_rwkv_body, seq=s),
        grid=(b, pairs),
        in_specs=[colv(0), colv(1), colv(2),
                  pl.BlockSpec((1, s, lo_w), lambda i, j: (i, 0, lo_blk)),
                  mucol(0), mucol(1), mucol(2),
                  pl.BlockSpec((2, lo_w), lambda i, j: (0, lo_blk)),
                  vec2, fac, vec2, fac,
                  pl.BlockSpec((LANES, LANES), lambda i, j: (0, j)),
                  vec, vec, vec, vec, vec],
        out_specs=pl.BlockSpec((1, s, LANES), lambda i, j: (i, 0, j)),
        out_shape=jax.ShapeDtypeStruct((b, s, width), F32),
        scratch_shapes=[seqbuf, seqbuf, seqbuf, pltpu.VMEM((s, lo_w), F32),
                        pltpu.VMEM((2, s, LANES), BF16), pltpu.VMEM((2, s, LANES), F32),
                        pltpu.VMEM((2, n_chunks, LANES, 3 * LANES), BF16),
                        pltpu.VMEM((2, n_chunks, LANES, LANES), F32),
                        seqbuf, seqbuf, seqbuf,
                        pltpu.VMEM((2, LANES, LANES), F32)],
        compiler_params=_cparams(("parallel", "parallel")),
        name="rwkv7",
    )(pb, pb, pb, pb, mu, mu, mu, mu, w0, w2p.astype(BF16), a0, a2p.astype(BF16), g2.astype(BF16),
      k_k.reshape(1, width), k_a.reshape(1, width), r_k.reshape(1, width),
      lnx_w.reshape(1, width), lnx_b.reshape(1, width))


def _fnet_body(u_ref, cs_ref, f_ref, o_ref, z_ref, *, seq, groups, tr):
    for g in range(groups):
        z = _mmb(u_ref[0, :, g * LANES:(g + 1) * LANES], cs_ref[...])
        z_ref[0:seq, g * LANES:(g + 1) * LANES] = z[:, :LANES].astype(BF16)
        z_ref[seq:2 * seq, g * LANES:(g + 1) * LANES] = z[:, LANES:].astype(BF16)
    scale = float(1.0 / np.sqrt(float(seq * LANES)))

    def rows(i, carry):
        t0 = pl.multiple_of(i * tr, tr)
        o_ref[0, pl.ds(t0, tr), :] = jnp.dot(f_ref[pl.ds(t0, tr), :], z_ref[...],
                                             preferred_element_type=F32) * scale
        return carry

    lax.fori_loop(0, seq // tr, rows, 0)


def _dft_tables(seq):
    def table(n):
        idx = jnp.arange(n, dtype=jnp.int32)
        ang = ((idx[:, None] * idx[None, :]) % n).astype(F32) * float(2.0 * np.pi / n)
        return jnp.cos(ang), jnp.sin(ang)
    cs, ss = table(seq)
    cc, sc = table(LANES)
    return (jnp.concatenate([cs, -ss], axis=1).astype(BF16),
            jnp.concatenate([cc, sc], axis=1).astype(BF16))


def _fnet(pc):
    b, s, w = pc.shape
    groups = w // LANES
    f_tab, c_tab = _dft_tables(s)
    tr = min(s, 512)
    return pl.pallas_call(
        functools.partial(_fnet_body, seq=s, groups=groups, tr=tr),
        grid=(b,),
        in_specs=[pl.BlockSpec((1, s, w), lambda i: (i, 0, 0)),
                  _const_spec((LANES, 2 * LANES)),
                  _const_spec((s, 2 * s))],
        out_specs=pl.BlockSpec((1, s, w), lambda i: (i, 0, 0)),
        out_shape=jax.ShapeDtypeStruct((b, s, w), F32),
        scratch_shapes=[pltpu.VMEM((2 * s, w), BF16)],
        compiler_params=_cparams(("parallel",)),
        name="fnet",
    )(pc, c_tab, f_tab)


def _conformer_body(u_ref, w_ref, b_ref, g_ref, beta_ref, o_ref, hp_ref, *, seq, width, tr):
    zeros = jnp.zeros((CONV_PAD, width), F32)
    hp_ref[0:CONV_PAD, :] = zeros
    hp_ref[CONV_PAD + seq:CONV_PAD + seq + CONV_PAD, :] = zeros

    def glu(i, carry):
        t0 = pl.multiple_of(i * tr, tr)
        val = u_ref[0, pl.ds(t0, tr), 0:width]
        gate = u_ref[0, pl.ds(t0, tr), width:2 * width]
        hp_ref[pl.ds(CONV_PAD + t0, tr), :] = val * _sigmoid(gate)
        return carry

    lax.fori_loop(0, seq // tr, glu, 0)

    def conv(i, carry):
        t0 = pl.multiple_of(i * tr, tr)
        win = hp_ref[pl.ds(t0, tr + 2 * CONV_PAD), :]
        acc = jnp.zeros((tr, width), F32) + b_ref[...]
        for sub in range(SUBLANES):
            rot = win if sub == 0 else pltpu.roll(win, tr + 2 * CONV_PAD - sub, 0)
            for j in range(CONV_K):
                off = j + CONV_PAD - CONV_K // 2
                if off % SUBLANES == sub:
                    base = off - sub
                    acc = acc + w_ref[j:j + 1, :] * rot[base:base + tr]
        mu = jnp.mean(acc, axis=-1, keepdims=True)
        xc = acc - mu
        y = xc * lax.rsqrt(jnp.mean(xc * xc, axis=-1, keepdims=True) + LN_EPS)
        o_ref[0, pl.ds(t0, tr), :] = _silu(y * g_ref[...] + beta_ref[...])
        return carry

    lax.fori_loop(0, seq // tr, conv, 0)


def _conformer(pd, conv_w, conv_b, ln_g, ln_b):
    b, s, w2 = pd.shape
    w = w2 // 2
    tr = 64
    vec = pl.BlockSpec((1, w), lambda i: (0, 0))
    return pl.pallas_call(
        functools.partial(_conformer_body, seq=s, width=w, tr=tr),
        grid=(b,),
        in_specs=[pl.BlockSpec((1, s, w2), lambda i: (i, 0, 0)),
                  pl.BlockSpec((CONV_K, w), lambda i: (0, 0)),
                  vec, vec, vec],
        out_specs=pl.BlockSpec((1, s, w), lambda i: (i, 0, 0)),
        out_shape=jax.ShapeDtypeStruct((b, s, w), F32),
        scratch_shapes=[pltpu.VMEM((s + 2 * CONV_PAD, w), F32)],
        compiler_params=_cparams(("parallel",)),
        name="conformer_conv",
    )(pd, conv_w, conv_b.reshape(1, w), ln_g.reshape(1, w), ln_b.reshape(1, w))


def _outproj_body(x_ref, ya_ref, yb_ref, gate_ref, w_ref, o_ref):
    mix = (jnp.dot(ya_ref[0].astype(BF16), w_ref[0], preferred_element_type=F32)
           + jnp.dot(yb_ref[0].astype(BF16), w_ref[1], preferred_element_type=F32))
    o_ref[0] = x_ref[0] + gate_ref[0] * mix


def _outproj(x, ya, yb, gate, w_out, tm):
    b, s, d = x.shape
    half = ya.shape[-1]
    tile = lambda n: pl.BlockSpec((1, tm, n), lambda i, j: (i, j, 0))
    return pl.pallas_call(
        _outproj_body,
        grid=(b, s // tm),
        in_specs=[tile(d), tile(half), tile(half),
                  pl.BlockSpec((1, 1, d), lambda i, j: (i, 0, 0)),
                  _const_spec((2, half, d))],
        out_specs=tile(d),
        out_shape=jax.ShapeDtypeStruct((b, s, d), F32),
        compiler_params=_cparams(("parallel", "parallel")),
        name="outproj",
    )(x, ya, yb, gate, w_out.reshape(2, half, d).astype(BF16))


def _ffn_body(x_ref, xp_ref, xn_ref, g_ref, sc_ref, sh_ref, gate_ref, wu_ref, wv_ref, cw_ref, cb_ref,
              wd_ref, fg_ref, o_ref, acc_ref, *, seq, tm, final_norm):
    rows = tm + 2 * HALO
    xa = jnp.concatenate([xp_ref[0], x_ref[0], xn_ref[0]], axis=0)
    hb = _modulated_norm(xa, g_ref[...], sc_ref[0], sh_ref[0]).astype(BF16)
    pos = pl.program_id(1) * tm - HALO + lax.broadcasted_iota(jnp.int32, (rows, 1), 0)
    inside = (pos >= 0) & (pos < seq)
    acc_ref[...] = jnp.zeros_like(acc_ref)

    def step(ci, carry):
        u = jnp.dot(hb, wu_ref[ci], preferred_element_type=F32)
        u = jnp.where(inside, u, 0.0)
        v = jnp.dot(hb[HALO:HALO + tm], wv_ref[ci], preferred_element_type=F32)
        cw = cw_ref[ci]
        u_prev = pltpu.roll(u, 1, 0)[HALO:HALO + tm]
        u_next = pltpu.roll(u, rows - 1, 0)[HALO:HALO + tm]
        uc = cw[0:1, :] * u_prev + cw[1:2, :] * u[HALO:HALO + tm] + cw[2:3, :] * u_next + cb_ref[ci]
        act = (_silu(uc) * v).astype(BF16)
        acc_ref[...] += jnp.dot(act, wd_ref[ci], preferred_element_type=F32)
        return carry

    lax.fori_loop(0, wu_ref.shape[0], step, 0)
    y = x_ref[0] + gate_ref[0] * acc_ref[...]
    if final_norm:
        ms = jnp.mean(y * y, axis=-1, keepdims=True)
        y = y * lax.rsqrt(ms + RMS_EPS) * fg_ref[...]
    o_ref[0] = y


def _ffn(x, g, scale, shift, gate, w_up, conv_w, conv_b, w_down, final_g, final_norm, tm):
    b, s, d = x.shape
    dff = w_down.shape[0]
    assert dff % FF_CHUNK == 0 and s % tm == 0 and tm % HALO == 0
    nch = dff // FF_CHUNK
    chunks = lambda w: w.reshape(d, nch, FF_CHUNK).transpose(1, 0, 2).astype(BF16)
    wu, wv = chunks(w_up[:, :dff]), chunks(w_up[:, dff:])
    cw = conv_w.reshape(conv_w.shape[0], nch, FF_CHUNK).transpose(1, 0, 2)
    cb = conv_b.reshape(nch, 1, FF_CHUNK)
    wd = w_down.reshape(nch, FF_CHUNK, d).astype(BF16)
    hb = tm // HALO
    last = s // HALO - 1
    row = pl.BlockSpec((1, 1, d), lambda i, j: (i, 0, 0))
    vec = pl.BlockSpec((1, d), lambda i, j: (0, 0))
    return pl.pallas_call(
        functools.partial(_ffn_body, seq=s, tm=tm, final_norm=final_norm),
        grid=(b, s // tm),
        in_specs=[pl.BlockSpec((1, tm, d), lambda i, j: (i, j, 0)),
                  pl.BlockSpec((1, HALO, d), lambda i, j: (i, jnp.maximum(j * hb - 1, 0), 0)),
                  pl.BlockSpec((1, HALO, d), lambda i, j: (i, jnp.minimum((j + 1) * hb, last), 0)),
                  vec, row, row, row,
                  _const_spec((nch, d, FF_CHUNK)), _const_spec((nch, d, FF_CHUNK)),
                  _const_spec((nch, conv_w.shape[0], FF_CHUNK)), _const_spec((nch, 1, FF_CHUNK)),
                  _const_spec((nch, FF_CHUNK, d)),
                  vec],
        out_specs=pl.BlockSpec((1, tm, d), lambda i, j: (i, j, 0)),
        out_shape=jax.ShapeDtypeStruct((b, s, d), F32),
        scratch_shapes=[pltpu.VMEM((tm, d), F32)],
        compiler_params=_cparams(("parallel", "parallel")),
        name="conv_ffn",
    )(x, x, x, g.reshape(1, d), scale, shift, gate, wu, wv, cw, cb, wd, final_g.reshape(1, d))


def kernel(x, c, ada_w, ada_b, norm_g, final_g, ab_w_in, ab_w_out, hgrn_gamma, hgrn_norm_g, rwkv_mu, rwkv_w0, rwkv_w2, rwkv_a0, rwkv_a2, rwkv_g2, rwkv_kk, rwkv_ka, rwkv_rk, rwkv_lnx_w, rwkv_lnx_b, cd_w_in, cd_w_out, dconv_w, dconv_b, dconv_ln_g, dconv_ln_b, ffn_w_up, ffn_conv_w, ffn_conv_b, ffn_w_down):
    bsz, seq, d = x.shape
    depth = ada_w.shape[0]
    a_in = 5 * (hgrn_gamma.shape[-1])
    c_width = dconv_w.shape[-1]
    tm = min(seq, 512)

    mod = _adaln(c, ada_w.reshape(depth * 2, d, 3 * d), ada_b.reshape(depth * 2, 3 * d))
    mod = mod.reshape(depth, 2, bsz, 3, 1, d)

    for l in range(depth):
        j = l // 2
        shift, scale, gate = mod[l, 0, :, 0], mod[l, 0, :, 1], mod[l, 0, :, 2]
        if l % 2 == 0:
            pa, pb = _inproj(x, norm_g[l, 0], scale, shift, ab_w_in[j],
                             (a_in, ab_w_in.shape[-1] - a_in), tm)
            y1 = _hgrn(pa, hgrn_gamma, hgrn_norm_g[j], l)
            y2 = _rwkv(pb, rwkv_mu[j], rwkv_w0[j], rwkv_w2[j], rwkv_a0[j], rwkv_a2[j], rwkv_g2[j],
                       rwkv_kk[j], rwkv_ka[j], rwkv_rk[j], rwkv_lnx_w[j], rwkv_lnx_b[j])
            w_out = ab_w_out[j]
        else:
            pc, pd = _inproj(x, norm_g[l, 0], scale, shift, cd_w_in[j],
                             (cd_w_in.shape[-1] - 2 * c_width, 2 * c_width), tm)
            y1 = _fnet(pc)
            y2 = _conformer(pd, dconv_w[j], dconv_b[j], dconv_ln_g[j], dconv_ln_b[j])
            w_out = cd_w_out[j]
        x = _outproj(x, y1, y2, gate, w_out, tm)
        shift, scale, gate = mod[l, 1, :, 0], mod[l, 1, :, 1], mod[l, 1, :, 2]
        x = _ffn(x, norm_g[l, 1], scale, shift, gate, ffn_w_up[l], ffn_conv_w[l], ffn_conv_b[l],
                 ffn_w_down[l], final_g, l == depth - 1, tm)
    return x
```

```python
import functools

import numpy as np
import jax
import jax.numpy as jnp
from jax import lax
from jax.experimental import pallas as pl
from jax.experimental.pallas import tpu as pltpu

F32 = jnp.float32
BF16 = jnp.bfloat16
HI = lax.Precision.HIGHEST

LANES = 128
SUBLANES = 8
VMEM_LIMIT_BYTES = 56 * 1024 * 1024

RMS_EPS = 1e-6
LN_EPS = 1e-5
RWKV_GN_EPS = 64e-5
RWKV_DH = 64
CONV_K = 31
CONV_PAD = 16
HGRN_C = 64
HGRN_NB = 4
RWKV_C = 128
RWKV_NB = 4
EPILOGUE_ROWS = 256
FF_CHUNK = 256
HALO = 2 * SUBLANES

NN = (((1,), (0,)), ((), ()))
NT = (((1,), (1,)), ((), ()))
TN = (((0,), (0,)), ((), ()))


def _mmf(a, b, dims=NN):
    return lax.dot_general(a, b, dims, precision=HI, preferred_element_type=F32)


def _mmb(a, b, dims=NN):
    return lax.dot_general(a.astype(BF16), b.astype(BF16), dims, preferred_element_type=F32)


def _sigmoid(x):
    return jax.nn.sigmoid(x)


def _silu(x):
    return x * jax.nn.sigmoid(x)


def _cparams(sem):
    return pltpu.CompilerParams(dimension_semantics=sem, vmem_limit_bytes=VMEM_LIMIT_BYTES)


def _const_spec(shape):
    nd = len(shape)
    return pl.BlockSpec(shape, lambda *_: (0,) * nd, pipeline_mode=pl.Buffered(1))


def _adaln_body(c_ref, w_ref, b_ref, o_ref):
    o_ref[0] = _mmf(_silu(c_ref[...]), w_ref[0]) + b_ref[0]


def _adaln(c, ada_w, ada_b):
    k, d, d3 = ada_w.shape
    b = c.shape[0]
    return pl.pallas_call(
        _adaln_body,
        grid=(k, d3 // d),
        in_specs=[pl.BlockSpec((b, d), lambda i, j: (0, 0)),
                  pl.BlockSpec((1, d, d), lambda i, j: (i, 0, j)),
                  pl.BlockSpec((1, 1, d), lambda i, j: (i, 0, j))],
        out_specs=pl.BlockSpec((1, b, d), lambda i, j: (i, 0, j)),
        out_shape=jax.ShapeDtypeStruct((k, b, d3), F32),
        compiler_params=_cparams(("parallel", "parallel")),
        name="adaln",
    )(c, ada_w, ada_b.reshape(k, 1, d3))


def _modulated_norm(x, g, scale, shift):
    ms = jnp.mean(x * x, axis=-1, keepdims=True)
    return (x * lax.rsqrt(ms + RMS_EPS) * g) * (1.0 + scale) + shift


def _inproj_body(x_ref, g_ref, sc_ref, sh_ref, w_ref, *o_refs):
    hb = _modulated_norm(x_ref[0], g_ref[...], sc_ref[0], sh_ref[0]).astype(BF16)
    off = 0
    for o_ref in o_refs:
        n = o_ref.shape[-1]
        o_ref[0] = jnp.dot(hb, w_ref[:, off:off + n], preferred_element_type=F32).astype(o_ref.dtype)
        off += n


def _inproj(x, g, scale, shift, w, splits, dtypes, tm):
    b, s, d = x.shape
    n = w.shape[1]
    assert sum(splits) == n and s % tm == 0
    row = pl.BlockSpec((1, 1, d), lambda i, j: (i, 0, 0))
    return pl.pallas_call(
        _inproj_body,
        grid=(b, s // tm),
        in_specs=[pl.BlockSpec((1, tm, d), lambda i, j: (i, j, 0)),
                  pl.BlockSpec((1, d), lambda i, j: (0, 0)),
                  row, row,
                  _const_spec((d, n))],
        out_specs=[pl.BlockSpec((1, tm, m), lambda i, j: (i, j, 0)) for m in splits],
        out_shape=[jax.ShapeDtypeStruct((b, s, m), dt) for m, dt in zip(splits, dtypes)],
        compiler_params=_cparams(("parallel", "parallel")),
        name="inproj",
    )(x, g.reshape(1, d), scale, shift, w.astype(BF16))


def _split2(x):
    hi = x.astype(BF16)
    return hi, (x - hi.astype(F32)).astype(BF16)


def _split3(x):
    hi = x.astype(BF16)
    rem = x - hi.astype(F32)
    mid = rem.astype(BF16)
    return hi, mid, (rem - mid.astype(F32)).astype(BF16)


def _mask_mm3(mask_bf16, x):
    n = x.shape[1]
    out = jnp.dot(mask_bf16, jnp.concatenate(_split3(x), axis=1), preferred_element_type=F32)
    return out[:, :n] + out[:, n:2 * n] + out[:, 2 * n:]


def _mm_mask2(x, mask_bf16):
    m = x.shape[0]
    out = jnp.dot(jnp.concatenate(_split2(x), axis=0), mask_bf16, preferred_element_type=F32)
    return out[:m] + out[m:]


def _hgrn_body(q_ref, ff_ref, fb_ref, i_ref, g_ref, gam_ref, ng_ref, o_ref,
               of_ref, ob_ref, st_ref, *, seq, layer):
    c = HGRN_C
    nb = HGRN_NB
    n_chunks = seq // c
    rows = [gam_ref[pl.ds(i, 1), :] for i in range(gam_ref.shape[0])]
    mx = functools.reduce(jnp.maximum, rows)
    es = [jnp.exp(r - mx) for r in rows]
    lb = sum(es[:layer + 1]) / sum(es)

    r_i = lax.broadcasted_iota(jnp.int32, (c, c), 0)
    c_i = lax.broadcasted_iota(jnp.int32, (c, c), 1)
    masks = [c_i <= r_i, c_i >= r_i]
    masks_bf = [jnp.where(m, 1.0, 0.0).astype(BF16) for m in masks]
    tot_rows = [c - 1, 0]
    fraw_refs = [ff_ref, fb_ref]
    out_refs = [of_ref, ob_ref]
    st_ref[...] = jnp.zeros_like(st_ref)

    def body(i, carry):
        probs = []
        for d in range(2):
            for u in range(nb):
                ci = i * nb + u
                probs.append((d, pl.multiple_of((ci if d == 0 else n_chunks - 1 - ci) * c, c)))
        q = [q_ref[0, pl.ds(t0, c), :] for d, t0 in probs]
        v = [i_ref[0, pl.ds(t0, c), :] for d, t0 in probs]
        f = [lb + (1.0 - lb) * _sigmoid(fraw_refs[d][0, pl.ds(t0, c), :]) for d, t0 in probs]
        cum = [_mask_mm3(masks_bf[d], jnp.log(fj)) for (d, t0), fj in zip(probs, f)]
        k = [1.0 - fj for fj in f]
        mid = [cj[c // 2:c // 2 + 1, :] for cj in cum]
        tot = [cj[tot_rows[d]:tot_rows[d] + 1, :] for (d, t0), cj in zip(probs, cum)]
        scores = [_mmb(qj * jnp.exp(cj - mj), kj * jnp.exp(mj - cj), NT)
                  for qj, kj, cj, mj in zip(q, k, cum, mid)]
        scores = [jnp.where(masks[d], sj, 0.0) for (d, t0), sj in zip(probs, scores)]
        intra = [_mmb(sj, vj) for sj, vj in zip(scores, v)]
        kv = [_mmb(vj, kj * jnp.exp(tj - cj), TN) for vj, kj, tj, cj in zip(v, k, tot, cum)]
        q_dec = [qj * jnp.exp(cj) for qj, cj in zip(q, cum)]
        dec = [jnp.exp(tj) for tj in tot]
        for d in range(2):
            st = st_ref[d]
            for j, (dj, t0) in enumerate(probs):
                if dj == d:
                    out_refs[d][pl.ds(t0, c), :] = intra[j] + _mmb(q_dec[j], st, NT)
                    st = st * dec[j] + kv[j]
            st_ref[d] = st
        return carry

    lax.fori_loop(0, n_chunks // nb, body, 0)

    te = min(seq, EPILOGUE_ROWS)

    def epilogue(i, carry):
        t0 = pl.multiple_of(i * te, te)
        o = of_ref[pl.ds(t0, te), :] + ob_ref[pl.ds(t0, te), :]
        ms = jnp.mean(o * o, axis=-1, keepdims=True)
        y = o * lax.rsqrt(ms + RMS_EPS) * ng_ref[...]
        o_ref[0, pl.ds(t0, te), :] = (y * _silu(g_ref[0, pl.ds(t0, te), :])).astype(o_ref.dtype)
        return carry

    lax.fori_loop(0, seq // te, epilogue, 0)


def _hgrn(pa, gamma, norm_g, layer):
    b, s, w5 = pa.shape
    heads = w5 // 5 // LANES
    assert s % (HGRN_C * HGRN_NB) == 0
    col = lambda k: pl.BlockSpec((1, s, LANES), lambda i, h, k=k: (i, 0, k * heads + h))
    return pl.pallas_call(
        functools.partial(_hgrn_body, seq=s, layer=layer),
        grid=(b, heads),
        in_specs=[col(0), col(1), col(2), col(3), col(4),
                  pl.BlockSpec((gamma.shape[0], LANES), lambda i, h: (0, h)),
                  pl.BlockSpec((1, LANES), lambda i, h: (0, 0))],
        out_specs=pl.BlockSpec((1, s, LANES), lambda i, h: (i, 0, h)),
        out_shape=jax.ShapeDtypeStruct((b, s, heads * LANES), BF16),
        scratch_shapes=[pltpu.VMEM((s, LANES), F32), pltpu.VMEM((s, LANES), F32),
                        pltpu.VMEM((2, LANES, LANES), F32)],
        compiler_params=_cparams(("parallel", "parallel")),
        name="hgrn2",
    )(pa, pa, pa, pa, pa, gamma, norm_g.reshape(1, LANES))


def _softplus(z):
    return jnp.maximum(z, 0.0) + jnp.log(1.0 + jnp.exp(-jnp.abs(z)))


def _tri_inverses(ns, r_i, c_i):
    size = ns[0].shape[0]
    blk = SUBLANES
    diag_blocks = r_i // blk == c_i // blk
    eye = jnp.where(r_i == c_i, 1.0, 0.0)
    n0 = [jnp.where(diag_blocks, n, 0.0).astype(BF16) for n in ns]
    t = [eye + n for n in n0]
    x = [jnp.dot(n, n, preferred_element_type=F32).astype(BF16) for n in n0]
    t = [tj + _mmb(tj, xj) for tj, xj in zip(t, x)]
    x = [jnp.dot(xj, xj, preferred_element_type=F32).astype(BF16) for xj in x]
    t = [tj + _mmb(tj, xj) for tj, xj in zip(t, x)]
    while blk < size:
        pair = (r_i // (2 * blk) == c_i // (2 * blk)) & (r_i // blk != c_i // blk)
        tb = [tj.astype(BF16) for tj in t]
        p = [jnp.dot(tj, jnp.where(pair, n, 0.0).astype(BF16), preferred_element_type=F32).astype(BF16)
             for tj, n in zip(tb, ns)]
        t = [tj + jnp.dot(pj, tbj, preferred_element_type=F32) for tj, pj, tbj in zip(t, p, tb)]
        blk *= 2
    return t


def _rwkv_body(r_ref, k_ref, v_ref, lo_ref, mur_ref, muk_ref, muv_ref, mulo_ref,
               w0_ref, w2_ref, a0_ref, a2_ref, g2_ref, kk_ref, ka_ref, rk_ref, lw_ref, lb_ref,
               o_ref,
               rs_ref, ks_ref, vs_ref, los_ref, qp_ref, y0_ref, m_ref, gm_ref, bon_ref,
               yf_ref, yb_ref, h_ref, *, seq):
    c = RWKV_C
    nb = RWKV_NB
    n_chunks = seq // c

    def shift(src, mu_ref, dst):
        p = src[0]
        rows = lax.broadcasted_iota(jnp.int32, p.shape, 0)
        prev = jnp.where(rows == 0, 0.0, pltpu.roll(p, 1, 0))
        nxt = jnp.where(rows == seq - 1, 0.0, pltpu.roll(p, seq - 1, 0))
        dst[...] = p + mu_ref[0:1, :] * (prev - p) + mu_ref[1:2, :] * (nxt - p)

    shift(r_ref, mur_ref, rs_ref)
    shift(k_ref, muk_ref, ks_ref)
    shift(v_ref, muv_ref, vs_ref)
    shift(lo_ref, mulo_ref, los_ref)

    r_i = lax.broadcasted_iota(jnp.int32, (c, c), 0)
    c_i = lax.broadcasted_iota(jnp.int32, (c, c), 1)
    eye = r_i == c_i
    same_head = r_i // RWKV_DH == c_i // RWKV_DH
    head_sum = jnp.where(same_head, 1.0, 0.0).astype(BF16)
    incl = [c_i <= r_i, c_i >= r_i]
    incl_bf = [jnp.where(m, 1.0, 0.0).astype(BF16) for m in incl]
    r_q = lax.broadcasted_iota(jnp.int32, (2 * c, 2 * c), 0)
    c_q = lax.broadcasted_iota(jnp.int32, (2 * c, 2 * c), 1)
    r_l, c_l = r_q % c, c_q % c
    diag_r = (r_q >= c) & (r_l == c_l)
    quad = [(c_l < r_l) | diag_r, (c_l > r_l) | diag_r]
    tot_rows = [c - 1, 0]
    lane = lax.broadcasted_iota(jnp.int32, (1, LANES), 1)
    head_lanes = [lane < RWKV_DH, lane >= RWKV_DH]
    lane2 = lax.broadcasted_iota(jnp.int32, (1, 2 * LANES), 1)
    first_head2 = (lane2 % LANES) < RWKV_DH
    zeros = jnp.zeros((c, LANES), F32)

    def phase_a(i, carry):
        probs = []
        for u in range(nb):
            ci = i * nb + u
            t0 = pl.multiple_of(ci * c, c)
            r = rs_ref[pl.ds(t0, c), :]
            k = ks_ref[pl.ds(t0, c), :]
            v = vs_ref[pl.ds(t0, c), :]
            lo = los_ref[pl.ds(t0, c), :]
            kk = k * kk_ref[...]
            kk = kk * lax.rsqrt(_mm_mask2(kk * kk, head_sum) + 1e-12)
            tanh_w = jnp.tanh(lo[:, 0:LANES])
            bonus = zeros
            for d in range(2):
                w = w0_ref[d:d + 1, :] + _mmb(tanh_w, w2_ref[d])
                wlog = -jnp.exp(-_softplus(-w) - 0.5)
                a = _sigmoid(a0_ref[d:d + 1, :] + _mmb(lo[:, LANES:2 * LANES], a2_ref[d]))
                kd = k * (1.0 + (a - 1.0) * ka_ref[...])
                bonus = bonus + _mm_mask2(r * kd * rk_ref[...], head_sum) * v
                cum = _mask_mm3(incl_bf[d], wlog)
                mid = cum[c // 2:c // 2 + 1, :]
                tot = cum[tot_rows[d]:tot_rows[d] + 1, :]
                e1 = jnp.exp(cum - mid)
                e2 = jnp.exp(mid - cum)
                em = jnp.exp(mid)
                et = jnp.exp(tot - mid)
                a_rel = -kk * e1 * jnp.exp(-wlog)
                r_rel = r * e1
                b_inv = kk * a * e2
                k_inv = kd * e2
                probs.append(dict(ci=ci, t0=t0, d=d, v=v, a_rel=a_rel, r_rel=r_rel,
                                  a_dec=a_rel * em, r_dec=r_rel * em,
                                  rhs=jnp.concatenate([b_inv, k_inv], axis=0).astype(BF16),
                                  ends=jnp.concatenate([b_inv * et, k_inv * et], axis=0).astype(BF16),
                                  gamma=jnp.exp(tot)))
            bon_ref[pl.ds(t0, c), :] = bonus

        chains = [(p, hl) for p in probs for hl in head_lanes]
        lhs = [jnp.concatenate([jnp.where(hl, p["a_rel"], 0.0), jnp.where(hl, p["r_rel"], 0.0)],
                               axis=0).astype(BF16) for p, hl in chains]
        blocks = [lax.dot_general(lj, p["rhs"], NT, preferred_element_type=F32)
                  for lj, (p, hl) in zip(lhs, chains)]
        blocks = [jnp.where(quad[p["d"]], bj, 0.0) for bj, (p, hl) in zip(blocks, chains)]
        a_ab = [bj[:c, :c] for bj in blocks]
        a_ak = [bj[:c, c:].astype(BF16) for bj in blocks]
        a_r = [bj[c:, :].astype(BF16) for bj in blocks]
        vb = [p["v"].astype(BF16) for p, hl in chains]
        av = [jnp.dot(aj, vj, preferred_element_type=F32) for aj, vj in zip(a_ak, vb)]
        t = _tri_inverses(a_ab, r_i, c_i)
        wu = [_mmb(tj, jnp.concatenate([p["a_dec"], avj], axis=1))
              for tj, avj, (p, hl) in zip(t, av, chains)]
        lower = [jnp.concatenate([zeros, p["v"]], axis=1).astype(BF16) for p in probs]
        qy = [jnp.dot(arj, jnp.concatenate([wuj.astype(BF16), lower[j // 2]], axis=0),
                      preferred_element_type=F32) for j, (arj, wuj) in enumerate(zip(a_r, wu))]
        for j, p in enumerate(probs):
            d, ci, t0 = p["d"], p["ci"], p["t0"]
            wu_p = jnp.where(first_head2, wu[2 * j], wu[2 * j + 1])
            qy_p = jnp.where(first_head2, qy[2 * j], qy[2 * j + 1])
            qp_ref[d, pl.ds(t0, c), :] = (p["r_dec"] + qy_p[:, :c]).astype(BF16)
            y0_ref[d, pl.ds(t0, c), :] = qy_p[:, c:]
            mg = lax.dot_general(p["ends"], jnp.concatenate([wu_p.astype(BF16), lower[j]], axis=0),
                                 TN, preferred_element_type=F32)
            m = jnp.where(same_head, mg[:, :c], 0.0) + jnp.where(eye, p["gamma"], 0.0)
            m_hi, m_lo = _split2(m)
            m_ref[d, ci] = jnp.concatenate([m_hi, m_hi, m_lo], axis=1)
            gm_ref[d, ci] = jnp.where(same_head, mg[:, c:], 0.0)
        return carry

    lax.fori_loop(0, n_chunks // nb, phase_a, 0)

    h_ref[...] = jnp.zeros_like(h_ref)

    def phase_b(i, carry):
        cis = [i, n_chunks - 1 - i]
        t0s = [pl.multiple_of(ci * c, c) for ci in cis]
        h0 = [h_ref[d] for d in range(2)]
        hs = [_split2(h) for h in h0]
        y = [jnp.dot(qp_ref[d, pl.ds(t0s[d], c), :], hs[d][0], preferred_element_type=F32)
             + y0_ref[d, pl.ds(t0s[d], c), :] for d in range(2)]
        hn = [jnp.dot(m_ref[d, cis[d]], jnp.concatenate([hs[d][0], hs[d][1], hs[d][0]], axis=0),
                      preferred_element_type=F32) + gm_ref[d, cis[d]] for d in range(2)]
        yf_ref[pl.ds(t0s[0], c), :] = y[0]
        yb_ref[pl.ds(t0s[1], c), :] = y[1]
        for d in range(2):
            h_ref[d] = hn[d]
        return carry

    lax.fori_loop(0, n_chunks, phase_b, 0)

    inv_dh = 1.0 / RWKV_DH

    te = min(seq, EPILOGUE_ROWS)

    def phase_c(i, carry):
        t0 = pl.multiple_of(i * te, te)
        y = yf_ref[pl.ds(t0, te), :] + yb_ref[pl.ds(t0, te), :]
        yc = y - _mm_mask2(y, head_sum) * inv_dh
        var = _mm_mask2(yc * yc, head_sum) * inv_dh
        yn = yc * lax.rsqrt(var + RWKV_GN_EPS) * lw_ref[...] + lb_ref[...]
        gate = _mmb(_sigmoid(los_ref[pl.ds(t0, te), 2 * LANES:3 * LANES]), g2_ref[...])
        o_ref[0, pl.ds(t0, te), :] = ((yn + bon_ref[pl.ds(t0, te), :]) * gate).astype(o_ref.dtype)
        return carry

    lax.fori_loop(0, seq // te, phase_c, 0)


def _rwkv(pb, mu, w0, w2, a0, a2, g2, k_k, k_a, r_k, lnx_w, lnx_b):
    b, s, n_in = pb.shape
    width = w0.shape[-1]
    pairs = width // LANES
    lo_w = n_in - 3 * width
    assert lo_w == 3 * LANES and (3 * width) % lo_w == 0 and s % (RWKV_C * RWKV_NB) == 0
    lo_blk = 3 * width // lo_w
    lora = w2.shape[1]
    n_chunks = s // RWKV_C
    zero = jnp.zeros((lora, width), F32)
    w2p = jnp.stack([jnp.concatenate([w2[0], zero]), jnp.concatenate([zero, w2[1]])])
    a2p = jnp.stack([jnp.concatenate([a2[0], zero]), jnp.concatenate([zero, a2[1]])])
    colv = lambda k: pl.BlockSpec((1, s, LANES), lambda i, j, k=k: (i, 0, k * pairs + j))
    mucol = lambda k: pl.BlockSpec((2, LANES), lambda i, j, k=k: (0, k * pairs + j))
    vec = pl.BlockSpec((1, LANES), lambda i, j: (0, j))
    vec2 = pl.BlockSpec((2, LANES), lambda i, j: (0, j))
    fac = pl.BlockSpec((2, LANES, LANES), lambda i, j: (0, 0, j))
    seqbuf = pltpu.VMEM((s, LANES), F32)
    return pl.pallas_call(
        functools.partial(_rwkv_body, seq=s),
        grid=(b, pairs),
        in_specs=[colv(0), colv(1), colv(2),
                  pl.BlockSpec((1, s, lo_w), lambda i, j: (i, 0, lo_blk)),
                  mucol(0), mucol(1), mucol(2),
                  pl.BlockSpec((2, lo_w), lambda i, j: (0, lo_blk)),
                  vec2, fac, vec2, fac,
                  pl.BlockSpec((LANES, LANES), lambda i, j: (0, j)),
                  vec, vec, vec, vec, vec],
        out_specs=pl.BlockSpec((1, s, LANES), lambda i, j: (i, 0, j)),
        out_shape=jax.ShapeDtypeStruct((b, s, width), BF16),
        scratch_shapes=[seqbuf, seqbuf, seqbuf, pltpu.VMEM((s, lo_w), F32),
                        pltpu.VMEM((2, s, LANES), BF16), pltpu.VMEM((2, s, LANES), F32),
                        pltpu.VMEM((2, n_chunks, LANES, 3 * LANES), BF16),
                        pltpu.VMEM((2, n_chunks, LANES, LANES), F32),
                        seqbuf, seqbuf, seqbuf,
                        pltpu.VMEM((2, LANES, LANES), F32)],
        compiler_params=_cparams(("parallel", "parallel")),
        name="rwkv7",
    )(pb, pb, pb, pb, mu, mu, mu, mu, w0, w2p.astype(BF16), a0, a2p.astype(BF16), g2.astype(BF16),
      k_k.reshape(1, width), k_a.reshape(1, width), r_k.reshape(1, width),
      lnx_w.reshape(1, width), lnx_b.reshape(1, width))


def _fnet_body(u_ref, cs_ref, f_ref, o_ref, z_ref, *, seq, groups, tr):
    for g in range(groups):
        z = _mmb(u_ref[0, :, g * LANES:(g + 1) * LANES], cs_ref[...])
        z_ref[0:seq, g * LANES:(g + 1) * LANES] = z[:, :LANES].astype(BF16)
        z_ref[seq:2 * seq, g * LANES:(g + 1) * LANES] = z[:, LANES:].astype(BF16)
    scale = float(1.0 / np.sqrt(float(seq * LANES)))

    def rows(i, carry):
        t0 = pl.multiple_of(i * tr, tr)
        o_ref[0, pl.ds(t0, tr), :] = (jnp.dot(f_ref[pl.ds(t0, tr), :], z_ref[...],
                                              preferred_element_type=F32) * scale).astype(o_ref.dtype)
        return carry

    lax.fori_loop(0, seq // tr, rows, 0)


def _dft_tables(seq):
    def table(n):
        idx = jnp.arange(n, dtype=jnp.int32)
        ang = ((idx[:, None] * idx[None, :]) % n).astype(F32) * float(2.0 * np.pi / n)
        return jnp.cos(ang), jnp.sin(ang)
    cs, ss = table(seq)
    cc, sc = table(LANES)
    return (jnp.concatenate([cs, -ss], axis=1).astype(BF16),
            jnp.concatenate([cc, sc], axis=1).astype(BF16))


def _fnet(pc):
    b, s, w = pc.shape
    groups = w // LANES
    f_tab, c_tab = _dft_tables(s)
    tr = min(s, 512)
    return pl.pallas_call(
        functools.partial(_fnet_body, seq=s, groups=groups, tr=tr),
        grid=(b,),
        in_specs=[pl.BlockSpec((1, s, w), lambda i: (i, 0, 0)),
                  _const_spec((LANES, 2 * LANES)),
                  _const_spec((s, 2 * s))],
        out_specs=pl.BlockSpec((1, s, w), lambda i: (i, 0, 0)),
        out_shape=jax.ShapeDtypeStruct((b, s, w), BF16),
        scratch_shapes=[pltpu.VMEM((2 * s, w), BF16)],
        compiler_params=_cparams(("parallel",)),
        name="fnet",
    )(pc, c_tab, f_tab)


def _conformer_body(u_ref, w_ref, b_ref, g_ref, beta_ref, o_ref, hp_ref, *, seq, width, tr):
    zeros = jnp.zeros((CONV_PAD, width), F32)
    hp_ref[0:CONV_PAD, :] = zeros
    hp_ref[CONV_PAD + seq:CONV_PAD + seq + CONV_PAD, :] = zeros

    def glu(i, carry):
        t0 = pl.multiple_of(i * tr, tr)
        val = u_ref[0, pl.ds(t0, tr), 0:width]
        gate = u_ref[0, pl.ds(t0, tr), width:2 * width]
        hp_ref[pl.ds(CONV_PAD + t0, tr), :] = val * _sigmoid(gate)
        return carry

    lax.fori_loop(0, seq // tr, glu, 0)

    def conv(i, carry):
        t0 = pl.multiple_of(i * tr, tr)
        cols = []
        for l0 in range(0, width, LANES):
            win = hp_ref[pl.ds(t0, tr + 2 * CONV_PAD), l0:l0 + LANES]
            acc = jnp.zeros((tr, LANES), F32) + b_ref[:, l0:l0 + LANES]
            for sub in range(SUBLANES):
                rot = win if sub == 0 else pltpu.roll(win, tr + 2 * CONV_PAD - sub, 0)
                for j in range(CONV_K):
                    off = j + CONV_PAD - CONV_K // 2
                    if off % SUBLANES == sub:
                        base = off - sub
                        acc = acc + w_ref[j:j + 1, l0:l0 + LANES] * rot[base:base + tr]
            cols.append(acc)
        acc = jnp.concatenate(cols, axis=1)
        mu = jnp.mean(acc, axis=-1, keepdims=True)
        xc = acc - mu
        y = xc * lax.rsqrt(jnp.mean(xc * xc, axis=-1, keepdims=True) + LN_EPS)
        o_ref[0, pl.ds(t0, tr), :] = _silu(y * g_ref[...] + beta_ref[...]).astype(o_ref.dtype)
        return carry

    lax.fori_loop(0, seq // tr, conv, 0)


def _conformer(pd, conv_w, conv_b, ln_g, ln_b):
    b, s, w2 = pd.shape
    w = w2 // 2
    tr = 64
    vec = pl.BlockSpec((1, w), lambda i: (0, 0))
    return pl.pallas_call(
        functools.partial(_conformer_body, seq=s, width=w, tr=tr),
        grid=(b,),
        in_specs=[pl.BlockSpec((1, s, w2), lambda i: (i, 0, 0)),
                  pl.BlockSpec((CONV_K, w), lambda i: (0, 0)),
                  vec, vec, vec],
        out_specs=pl.BlockSpec((1, s, w), lambda i: (i, 0, 0)),
        out_shape=jax.ShapeDtypeStruct((b, s, w), BF16),
        scratch_shapes=[pltpu.VMEM((s + 2 * CONV_PAD, w), F32)],
        compiler_params=_cparams(("parallel",)),
        name="conformer_conv",
    )(pd, conv_w, conv_b.reshape(1, w), ln_g.reshape(1, w), ln_b.reshape(1, w))


def _mix_ffn_body(x_ref, xp_ref, xn_ref, ya_ref, yap_ref, yan_ref, yb_ref, ybp_ref, ybn_ref,
                  mgate_ref, wo_ref, g_ref, sc_ref, sh_ref, gate_ref, wup_ref, cw_ref, cb_ref,
                  wd_ref, fg_ref, o_ref, act_ref, *, seq, tm, final_norm):
    rows = tm + 2 * HALO
    dff = wd_ref.shape[0]
    with_halo = lambda p, m, n: jnp.concatenate([p[0], m[0], n[0]], axis=0)
    mix = (jnp.dot(with_halo(yap_ref, ya_ref, yan_ref), wo_ref[0], preferred_element_type=F32)
           + jnp.dot(with_halo(ybp_ref, yb_ref, ybn_ref), wo_ref[1], preferred_element_type=F32))
    xa = with_halo(xp_ref, x_ref, xn_ref) + mgate_ref[0] * mix
    x_mid = xa[HALO:HALO + tm]
    hb = _modulated_norm(xa, g_ref[...], sc_ref[0], sh_ref[0]).astype(BF16)
    pos = pl.program_id(1) * tm - HALO + lax.broadcasted_iota(jnp.int32, (rows, 1), 0)
    inside = (pos >= 0) & (pos < seq)

    for lo in range(0, dff, FF_CHUNK):
        hi = lo + FF_CHUNK
        u = jnp.dot(hb, wup_ref[:, lo:hi], preferred_element_type=F32)
        u = jnp.where(inside, u, 0.0)
        v = jnp.dot(hb[HALO:HALO + tm], wup_ref[:, dff + lo:dff + hi], preferred_element_type=F32)
        u_prev = pltpu.roll(u, 1, 0)[HALO:HALO + tm]
        u_next = pltpu.roll(u, rows - 1, 0)[HALO:HALO + tm]
        uc = (cw_ref[0:1, lo:hi] * u_prev + cw_ref[1:2, lo:hi] * u[HALO:HALO + tm]
              + cw_ref[2:3, lo:hi] * u_next + cb_ref[:, lo:hi])
        act_ref[:, lo:hi] = (_silu(uc) * v).astype(BF16)
    y = x_mid + gate_ref[0] * jnp.dot(act_ref[...], wd_ref[...], preferred_element_type=F32)
    if final_norm:
        ms = jnp.mean(y * y, axis=-1, keepdims=True)
        y = y * lax.rsqrt(ms + RMS_EPS) * fg_ref[...]
    o_ref[0] = y


def _mix_ffn(x, ya, yb, mix_gate, w_out, g, scale, shift, gate, w_up, conv_w, conv_b, w_down,
             final_g, final_norm, tm):
    b, s, d = x.shape
    dff = w_down.shape[0]
    half = ya.shape[-1]
    assert dff % FF_CHUNK == 0 and s % tm == 0 and tm % HALO == 0
    hb = tm // HALO
    last = s // HALO - 1
    tiles = lambda n: [pl.BlockSpec((1, tm, n), lambda i, j: (i, j, 0)),
                       pl.BlockSpec((1, HALO, n), lambda i, j: (i, jnp.maximum(j * hb - 1, 0), 0)),
                       pl.BlockSpec((1, HALO, n), lambda i, j: (i, jnp.minimum((j + 1) * hb, last), 0))]
    row = pl.BlockSpec((1, 1, d), lambda i, j: (i, 0, 0))
    vec = pl.BlockSpec((1, d), lambda i, j: (0, 0))
    return pl.pallas_call(
        functools.partial(_mix_ffn_body, seq=s, tm=tm, final_norm=final_norm),
        grid=(b, s // tm),
        in_specs=[*tiles(d), *tiles(half), *tiles(half),
                  row, _const_spec((2, half, d)),
                  vec, row, row, row,
                  _const_spec((d, 2 * dff)),
                  _const_spec((conv_w.shape[0], dff)), _const_spec((1, dff)),
                  _const_spec((dff, d)),
                  vec],
        out_specs=pl.BlockSpec((1, tm, d), lambda i, j: (i, j, 0)),
        out_shape=jax.ShapeDtypeStruct((b, s, d), F32),
        scratch_shapes=[pltpu.VMEM((tm, dff), BF16)],
        compiler_params=_cparams(("parallel", "parallel")),
        name="mix_ffn",
    )(x, x, x, ya, ya, ya, yb, yb, yb, mix_gate, w_out.reshape(2, half, d).astype(BF16),
      g.reshape(1, d), scale, shift, gate, w_up.astype(BF16), conv_w, conv_b.reshape(1, dff),
      w_down.astype(BF16), final_g.reshape(1, d))


def kernel(x, c, ada_w, ada_b, norm_g, final_g, ab_w_in, ab_w_out, hgrn_gamma, hgrn_norm_g, rwkv_mu, rwkv_w0, rwkv_w2, rwkv_a0, rwkv_a2, rwkv_g2, rwkv_kk, rwkv_ka, rwkv_rk, rwkv_lnx_w, rwkv_lnx_b, cd_w_in, cd_w_out, dconv_w, dconv_b, dconv_ln_g, dconv_ln_b, ffn_w_up, ffn_conv_w, ffn_conv_b, ffn_w_down):
    bsz, seq, d = x.shape
    depth = ada_w.shape[0]
    a_in = 5 * (hgrn_gamma.shape[-1])
    c_width = dconv_w.shape[-1]
    tm = min(seq, 512)

    mod = _adaln(c, ada_w.reshape(depth * 2, d, 3 * d), ada_b.reshape(depth * 2, 3 * d))
    mod = mod.reshape(depth, 2, bsz, 3, 1, d)

    for l in range(depth):
        j = l // 2
        shift, scale, gate = mod[l, 0, :, 0], mod[l, 0, :, 1], mod[l, 0, :, 2]
        if l % 2 == 0:
            pa, pb = _inproj(x, norm_g[l, 0], scale, shift, ab_w_in[j],
                             (a_in, ab_w_in.shape[-1] - a_in), (F32, F32), tm)
            y1 = _hgrn(pa, hgrn_gamma, hgrn_norm_g[j], l)
            y2 = _rwkv(pb, rwkv_mu[j], rwkv_w0[j], rwkv_w2[j], rwkv_a0[j], rwkv_a2[j], rwkv_g2[j],
                       rwkv_kk[j], rwkv_ka[j], rwkv_rk[j], rwkv_lnx_w[j], rwkv_lnx_b[j])
            w_out = ab_w_out[j]
        else:
            pc, pd = _inproj(x, norm_g[l, 0], scale, shift, cd_w_in[j],
                             (cd_w_in.shape[-1] - 2 * c_width, 2 * c_width), (BF16, F32), tm)
            y1 = _fnet(pc)
            y2 = _conformer(pd, dconv_w[j], dconv_b[j], dconv_ln_g[j], dconv_ln_b[j])
            w_out = cd_w_out[j]
        shift, scale, ffn_gate = mod[l, 1, :, 0], mod[l, 1, :, 1], mod[l, 1, :, 2]
        x = _mix_ffn(x, y1, y2, gate, w_out, norm_g[l, 1], scale, shift, ffn_gate, ffn_w_up[l],
                     ffn_conv_w[l], ffn_conv_b[l], ffn_w_down[l], final_g, l == depth - 1, tm)
    return x
```

```python
import functools

import numpy as np
import jax
import jax.numpy as jnp
from jax import lax
from jax.experimental import pallas as pl
from jax.experimental.pallas import tpu as pltpu

F32 = jnp.float32
BF16 = jnp.bfloat16
HI = lax.Precision.HIGHEST

LANES = 128
SUBLANES = 8
VMEM_LIMIT_BYTES = 56 * 1024 * 1024

RMS_EPS = 1e-6
LN_EPS = 1e-5
RWKV_GN_EPS = 64e-5
RWKV_DH = 64
CONV_K = 31
CONV_PAD = 16
HGRN_C = 64
HGRN_NB = 8
RWKV_C = 128
RWKV_NB = 4
EPILOGUE_ROWS = 256
FF_CHUNK = 256
HALO = 2 * SUBLANES

NN = (((1,), (0,)), ((), ()))
NT = (((1,), (1,)), ((), ()))
TN = (((0,), (0,)), ((), ()))


def _mmf(a, b, dims=NN):
    return lax.dot_general(a, b, dims, precision=HI, preferred_element_type=F32)


def _mmb(a, b, dims=NN):
    return lax.dot_general(a.astype(BF16), b.astype(BF16), dims, preferred_element_type=F32)


def _sigmoid(x):
    return jax.nn.sigmoid(x)


def _silu(x):
    return x * jax.nn.sigmoid(x)


def _cparams(sem):
    return pltpu.CompilerParams(dimension_semantics=sem, vmem_limit_bytes=VMEM_LIMIT_BYTES)


def _const_spec(shape):
    nd = len(shape)
    return pl.BlockSpec(shape, lambda *_: (0,) * nd, pipeline_mode=pl.Buffered(1))


def _adaln_body(c_ref, w_ref, b_ref, o_ref):
    o_ref[0] = _mmf(_silu(c_ref[...]), w_ref[0]) + b_ref[0]


def _adaln(c, ada_w, ada_b):
    k, d, d3 = ada_w.shape
    b = c.shape[0]
    return pl.pallas_call(
        _adaln_body,
        grid=(k, d3 // d),
        in_specs=[pl.BlockSpec((b, d), lambda i, j: (0, 0)),
                  pl.BlockSpec((1, d, d), lambda i, j: (i, 0, j)),
                  pl.BlockSpec((1, 1, d), lambda i, j: (i, 0, j))],
        out_specs=pl.BlockSpec((1, b, d), lambda i, j: (i, 0, j)),
        out_shape=jax.ShapeDtypeStruct((k, b, d3), F32),
        compiler_params=_cparams(("parallel", "parallel")),
        name="adaln",
    )(c, ada_w, ada_b.reshape(k, 1, d3))


def _modulated_norm(x, g, scale, shift):
    ms = jnp.mean(x * x, axis=-1, keepdims=True)
    return (x * lax.rsqrt(ms + RMS_EPS) * g) * (1.0 + scale) + shift


def _inproj_body(x_ref, g_ref, sc_ref, sh_ref, w_ref, *o_refs):
    hb = _modulated_norm(x_ref[0], g_ref[...], sc_ref[0], sh_ref[0]).astype(BF16)
    off = 0
    for o_ref in o_refs:
        n = o_ref.shape[-1]
        o_ref[0] = jnp.dot(hb, w_ref[:, off:off + n], preferred_element_type=F32).astype(o_ref.dtype)
        off += n


def _inproj(x, g, scale, shift, w, splits, dtypes, tm):
    b, s, d = x.shape
    n = w.shape[1]
    assert sum(splits) == n and s % tm == 0
    row = pl.BlockSpec((1, 1, d), lambda i, j: (i, 0, 0))
    return pl.pallas_call(
        _inproj_body,
        grid=(b, s // tm),
        in_specs=[pl.BlockSpec((1, tm, d), lambda i, j: (i, j, 0)),
                  pl.BlockSpec((1, d), lambda i, j: (0, 0)),
                  row, row,
                  _const_spec((d, n))],
        out_specs=[pl.BlockSpec((1, tm, m), lambda i, j: (i, j, 0)) for m in splits],
        out_shape=[jax.ShapeDtypeStruct((b, s, m), dt) for m, dt in zip(splits, dtypes)],
        compiler_params=_cparams(("parallel", "parallel")),
        name="inproj",
    )(x, g.reshape(1, d), scale, shift, w.astype(BF16))


def _split2(x):
    hi = x.astype(BF16)
    return hi, (x - hi.astype(F32)).astype(BF16)


def _split3(x):
    hi = x.astype(BF16)
    rem = x - hi.astype(F32)
    mid = rem.astype(BF16)
    return hi, mid, (rem - mid.astype(F32)).astype(BF16)


def _mask_mm3(mask_bf16, x):
    n = x.shape[1]
    out = jnp.dot(mask_bf16, jnp.concatenate(_split3(x), axis=1), preferred_element_type=F32)
    return out[:, :n] + out[:, n:2 * n] + out[:, 2 * n:]


def _mm_mask2(x, mask_bf16):
    m = x.shape[0]
    out = jnp.dot(jnp.concatenate(_split2(x), axis=0), mask_bf16, preferred_element_type=F32)
    return out[:m] + out[m:]


def _hgrn_body(q_ref, ff_ref, fb_ref, i_ref, g_ref, gam_ref, ng_ref, o_ref,
               of_ref, ob_ref, st_ref, *, seq, layer):
    c = HGRN_C
    nb = HGRN_NB
    n_chunks = seq // c
    rows = [gam_ref[pl.ds(i, 1), :] for i in range(gam_ref.shape[0])]
    mx = functools.reduce(jnp.maximum, rows)
    es = [jnp.exp(r - mx) for r in rows]
    lb = sum(es[:layer + 1]) / sum(es)

    r_i = lax.broadcasted_iota(jnp.int32, (c, c), 0)
    c_i = lax.broadcasted_iota(jnp.int32, (c, c), 1)
    masks = [c_i <= r_i, c_i >= r_i]
    masks_bf = [jnp.where(m, 1.0, 0.0).astype(BF16) for m in masks]
    tot_rows = [c - 1, 0]
    fraw_refs = [ff_ref, fb_ref]
    out_refs = [of_ref, ob_ref]
    st_ref[...] = jnp.zeros_like(st_ref)

    def body(i, carry):
        probs = []
        for d in range(2):
            for u in range(nb):
                ci = i * nb + u
                probs.append((d, pl.multiple_of((ci if d == 0 else n_chunks - 1 - ci) * c, c)))
        q = [q_ref[0, pl.ds(t0, c), :] for d, t0 in probs]
        v = [i_ref[0, pl.ds(t0, c), :] for d, t0 in probs]
        f = [lb + (1.0 - lb) * _sigmoid(fraw_refs[d][0, pl.ds(t0, c), :]) for d, t0 in probs]
        cum = [_mask_mm3(masks_bf[d], jnp.log(fj)) for (d, t0), fj in zip(probs, f)]
        k = [1.0 - fj for fj in f]
        mid = [cj[c // 2:c // 2 + 1, :] for cj in cum]
        tot = [cj[tot_rows[d]:tot_rows[d] + 1, :] for (d, t0), cj in zip(probs, cum)]
        scores = [_mmb(qj * jnp.exp(cj - mj), kj * jnp.exp(mj - cj), NT)
                  for qj, kj, cj, mj in zip(q, k, cum, mid)]
        scores = [jnp.where(masks[d], sj, 0.0) for (d, t0), sj in zip(probs, scores)]
        intra = [_mmb(sj, vj) for sj, vj in zip(scores, v)]
        kv = [_mmb(vj, kj * jnp.exp(tj - cj), TN) for vj, kj, tj, cj in zip(v, k, tot, cum)]
        q_dec = [qj * jnp.exp(cj) for qj, cj in zip(q, cum)]
        dec = [jnp.exp(tj) for tj in tot]
        for d in range(2):
            st = st_ref[d]
            for j, (dj, t0) in enumerate(probs):
                if dj == d:
                    out_refs[d][pl.ds(t0, c), :] = intra[j] + _mmb(q_dec[j], st, NT)
                    st = st * dec[j] + kv[j]
            st_ref[d] = st
        return carry

    lax.fori_loop(0, n_chunks // nb, body, 0)

    te = min(seq, EPILOGUE_ROWS)

    def epilogue(i, carry):
        t0 = pl.multiple_of(i * te, te)
        o = of_ref[pl.ds(t0, te), :] + ob_ref[pl.ds(t0, te), :]
        ms = jnp.mean(o * o, axis=-1, keepdims=True)
        y = o * lax.rsqrt(ms + RMS_EPS) * ng_ref[...]
        o_ref[0, pl.ds(t0, te), :] = (y * _silu(g_ref[0, pl.ds(t0, te), :])).astype(o_ref.dtype)
        return carry

    lax.fori_loop(0, seq // te, epilogue, 0)


def _hgrn(pa, gamma, norm_g, layer):
    b, s, w5 = pa.shape
    heads = w5 // 5 // LANES
    assert s % (HGRN_C * HGRN_NB) == 0
    col = lambda k: pl.BlockSpec((1, s, LANES), lambda i, h, k=k: (i, 0, k * heads + h))
    return pl.pallas_call(
        functools.partial(_hgrn_body, seq=s, layer=layer),
        grid=(b, heads),
        in_specs=[col(0), col(1), col(2), col(3), col(4),
                  pl.BlockSpec((gamma.shape[0], LANES), lambda i, h: (0, h)),
                  pl.BlockSpec((1, LANES), lambda i, h: (0, 0))],
        out_specs=pl.BlockSpec((1, s, LANES), lambda i, h: (i, 0, h)),
        out_shape=jax.ShapeDtypeStruct((b, s, heads * LANES), BF16),
        scratch_shapes=[pltpu.VMEM((s, LANES), F32), pltpu.VMEM((s, LANES), F32),
                        pltpu.VMEM((2, LANES, LANES), F32)],
        compiler_params=_cparams(("parallel", "parallel")),
        name="hgrn2",
    )(pa, pa, pa, pa, pa, gamma, norm_g.reshape(1, LANES))


def _softplus(z):
    return jnp.maximum(z, 0.0) + jnp.log(1.0 + jnp.exp(-jnp.abs(z)))


def _tri_inverses(ns, r_i, c_i):
    size = ns[0].shape[0]
    blk = SUBLANES
    diag_blocks = r_i // blk == c_i // blk
    eye = jnp.where(r_i == c_i, 1.0, 0.0)
    n0 = [jnp.where(diag_blocks, n, 0.0).astype(BF16) for n in ns]
    t = [eye + n for n in n0]
    x = [jnp.dot(n, n, preferred_element_type=F32).astype(BF16) for n in n0]
    t = [tj + _mmb(tj, xj) for tj, xj in zip(t, x)]
    x = [jnp.dot(xj, xj, preferred_element_type=F32).astype(BF16) for xj in x]
    t = [tj + _mmb(tj, xj) for tj, xj in zip(t, x)]
    while blk < size:
        pair = (r_i // (2 * blk) == c_i // (2 * blk)) & (r_i // blk != c_i // blk)
        tb = [tj.astype(BF16) for tj in t]
        p = [jnp.dot(tj, jnp.where(pair, n, 0.0).astype(BF16), preferred_element_type=F32).astype(BF16)
             for tj, n in zip(tb, ns)]
        t = [tj + jnp.dot(pj, tbj, preferred_element_type=F32) for tj, pj, tbj in zip(t, p, tb)]
        blk *= 2
    return t


def _rwkv_body(r_ref, k_ref, v_ref, lo_ref, mur_ref, muk_ref, muv_ref, mulo_ref,
               w0_ref, w2_ref, a0_ref, a2_ref, g2_ref, kk_ref, ka_ref, rk_ref, lw_ref, lb_ref,
               o_ref,
               gds_ref, qp_ref, y0_ref, m_ref, gm_ref, bon_ref,
               yf_ref, yb_ref, h_ref, lhs_ref, rhs_ref, end_ref, adec_ref, rdec_ref, gam_ref, vb_ref, *, seq):
    c = RWKV_C
    nb = RWKV_NB
    n_chunks = seq // c
    n_groups = n_chunks // nb

    def shifted(src, mu_ref, t0):
        cur = src[0, pl.ds(t0, c), :]
        before = src[0, pl.ds(pl.multiple_of(jnp.maximum(t0 - SUBLANES, 0), SUBLANES), SUBLANES), :]
        after = src[0, pl.ds(pl.multiple_of(jnp.minimum(t0 + c, seq - SUBLANES), SUBLANES), SUBLANES), :]
        row_before = before[SUBLANES - 1:SUBLANES, :] * (t0 > 0).astype(F32)
        row_after = after[0:1, :] * (t0 + c < seq).astype(F32)
        rows = lax.broadcasted_iota(jnp.int32, cur.shape, 0)
        prev = jnp.where(rows == 0, row_before, pltpu.roll(cur, 1, 0))
        nxt = jnp.where(rows == c - 1, row_after, pltpu.roll(cur, c - 1, 0))
        return cur + mu_ref[0:1, :] * (prev - cur) + mu_ref[1:2, :] * (nxt - cur)

    r_i = lax.broadcasted_iota(jnp.int32, (c, c), 0)
    c_i = lax.broadcasted_iota(jnp.int32, (c, c), 1)
    eye = r_i == c_i
    same_head = r_i // RWKV_DH == c_i // RWKV_DH
    head_sum = jnp.where(same_head, 1.0, 0.0).astype(BF16)
    incl = [c_i <= r_i, c_i >= r_i]
    incl_bf = [jnp.where(m, 1.0, 0.0).astype(BF16) for m in incl]
    r_q = lax.broadcasted_iota(jnp.int32, (2 * c, 2 * c), 0)
    c_q = lax.broadcasted_iota(jnp.int32, (2 * c, 2 * c), 1)
    r_l, c_l = r_q % c, c_q % c
    diag_r = (r_q >= c) & (r_l == c_l)
    quad = [(c_l < r_l) | diag_r, (c_l > r_l) | diag_r]
    tot_rows = [c - 1, 0]
    lane = lax.broadcasted_iota(jnp.int32, (1, LANES), 1)
    head_lanes = [lane < RWKV_DH, lane >= RWKV_DH]
    lane2 = lax.broadcasted_iota(jnp.int32, (1, 2 * LANES), 1)
    first_head2 = (lane2 % LANES) < RWKV_DH
    zeros = jnp.zeros((c, LANES), F32)

    def prep(grp):
        for u in range(nb):
            ci = grp * nb + u
            t0 = pl.multiple_of(ci * c, c)
            r = shifted(r_ref, mur_ref, t0)
            k = shifted(k_ref, muk_ref, t0)
            v = shifted(v_ref, muv_ref, t0)
            lo = shifted(lo_ref, mulo_ref, t0)
            gds_ref[pl.ds(t0, c), :] = lo[:, 2 * LANES:3 * LANES]
            kk = k * kk_ref[...]
            kk = kk * lax.rsqrt(_mm_mask2(kk * kk, head_sum) + 1e-12)
            tanh_w = jnp.tanh(lo[:, 0:LANES])
            bonus = zeros
            for d in range(2):
                w = w0_ref[d:d + 1, :] + _mmb(tanh_w, w2_ref[d])
                wlog = -jnp.exp(-_softplus(-w) - 0.5)
                a = _sigmoid(a0_ref[d:d + 1, :] + _mmb(lo[:, LANES:2 * LANES], a2_ref[d]))
                kd = k * (1.0 + (a - 1.0) * ka_ref[...])
                bonus = bonus + _mm_mask2(r * kd * rk_ref[...], head_sum) * v
                cum = _mask_mm3(incl_bf[d], wlog)
                mid = cum[c // 2:c // 2 + 1, :]
                tot = cum[tot_rows[d]:tot_rows[d] + 1, :]
                e1 = jnp.exp(cum - mid)
                e2 = jnp.exp(mid - cum)
                em = jnp.exp(mid)
                et = jnp.exp(tot - mid)
                a_rel = -kk * e1 * jnp.exp(-wlog)
                r_rel = r * e1
                b_inv = kk * a * e2
                k_inv = kd * e2
                j = 2 * u + d
                for h, hl in enumerate(head_lanes):
                    lhs_ref[j, h] = jnp.concatenate([jnp.where(hl, a_rel, 0.0), jnp.where(hl, r_rel, 0.0)],
                                                    axis=0).astype(BF16)
                rhs_ref[j] = jnp.concatenate([b_inv, k_inv], axis=0).astype(BF16)
                end_ref[j] = jnp.concatenate([b_inv * et, k_inv * et], axis=0).astype(BF16)
                adec_ref[j] = (a_rel * em).astype(BF16)
                rdec_ref[j] = r_rel * em
                gam_ref[j] = jnp.exp(tot)
            vb_ref[u] = v.astype(BF16)
            bon_ref[pl.ds(t0, c), :] = bonus

    def chain(grp):
        probs =[dict(ci=grp * nb + u, t0=pl.multiple_of((grp * nb + u) * c, c), d=d, u=u, j=2 * u + d)
                 for u in range(nb) for d in range(2)]
        chains = [(p, h) for p in probs for h in range(2)]
        blocks = [lax.dot_general(lhs_ref[p["j"], h], rhs_ref[p["j"]], NT, preferred_element_type=F32)
                  for p, h in chains]
        blocks = [jnp.where(quad[p["d"]], bj, 0.0) for bj, (p, h) in zip(blocks, chains)]
        a_ab = [bj[:c, :c] for bj in blocks]
        a_ak = [bj[:c, c:].astype(BF16) for bj in blocks]
        a_r = [bj[c:, :].astype(BF16) for bj in blocks]
        av = [jnp.dot(aj, vb_ref[p["u"]], preferred_element_type=F32) for aj, (p, h) in zip(a_ak, chains)]
        t = _tri_inverses(a_ab, r_i, c_i)
        wu = [jnp.dot(tj.astype(BF16), jnp.concatenate([adec_ref[p["j"]], avj.astype(BF16)], axis=1),
                      preferred_element_type=F32) for tj, avj, (p, h) in zip(t, av, chains)]
        lower = [jnp.concatenate([zeros.astype(BF16), vb_ref[p["u"]]], axis=1) for p in probs]
        qy = [jnp.dot(arj, jnp.concatenate([wuj.astype(BF16), lower[j // 2]], axis=0),
                      preferred_element_type=F32) for j, (arj, wuj) in enumerate(zip(a_r, wu))]
        for j, p in enumerate(probs):
            d, ci, t0 = p["d"], p["ci"], p["t0"]
            wu_p = jnp.where(first_head2, wu[2 * j], wu[2 * j + 1])
            qy_p = jnp.where(first_head2, qy[2 * j], qy[2 * j + 1])
            qp_ref[d, pl.ds(t0, c), :] = (rdec_ref[j] + qy_p[:, :c]).astype(BF16)
            y0_ref[d, pl.ds(t0, c), :] = qy_p[:, c:]
            mg = lax.dot_general(end_ref[j], jnp.concatenate([wu_p.astype(BF16), lower[j]], axis=0),
                                 TN, preferred_element_type=F32)
            m = jnp.where(same_head, mg[:, :c], 0.0) + jnp.where(eye, gam_ref[j], 0.0)
            m_hi, m_lo = _split2(m)
            m_ref[d, ci] = jnp.concatenate([m_hi, m_hi, m_lo], axis=1)
            gm_ref[d, ci] = jnp.where(same_head, mg[:, c:], 0.0)

    def phase_a(grp, carry):
        prep(grp)
        chain(grp)
        return carry

    lax.fori_loop(0, n_groups, phase_a, 0)

    h_ref[...] = jnp.zeros_like(h_ref)
    inv_dh = 1.0 / RWKV_DH

    def finish(ys, t0s):
        yc = [y - _mm_mask2(y, head_sum) * inv_dh for y in ys]
        var = [_mm_mask2(x * x, head_sum) * inv_dh for x in yc]
        gate = [_mmb(_sigmoid(gds_ref[pl.ds(t0, c), :]), g2_ref[...]) for t0 in t0s]
        for x, vr, gt, t0 in zip(yc, var, gate, t0s):
            yn = x * lax.rsqrt(vr + RWKV_GN_EPS) * lw_ref[...] + lb_ref[...]
            o_ref[0, pl.ds(t0, c), :] = ((yn + bon_ref[pl.ds(t0, c), :]) * gt).astype(o_ref.dtype)

    def state_step(i, second_half):
        cis = [i, n_chunks - 1 - i]
        t0s = [pl.multiple_of(ci * c, c) for ci in cis]
        hs = [_split2(h_ref[d]) for d in range(2)]
        y = [jnp.dot(qp_ref[d, pl.ds(t0s[d], c), :], hs[d][0], preferred_element_type=F32)
             + y0_ref[d, pl.ds(t0s[d], c), :] for d in range(2)]
        hn = [jnp.dot(m_ref[d, cis[d]], jnp.concatenate([hs[d][0], hs[d][1], hs[d][0]], axis=0),
                      preferred_element_type=F32) + gm_ref[d, cis[d]] for d in range(2)]
        for d in range(2):
            h_ref[d] = hn[d]
        if second_half:
            finish([y[0] + yb_ref[pl.ds(t0s[0], c), :], yf_ref[pl.ds(t0s[1], c), :] + y[1]], t0s)
        else:
            yf_ref[pl.ds(t0s[0], c), :] = y[0]
            yb_ref[pl.ds(t0s[1], c), :] = y[1]

    def first_half(i, carry):
        state_step(i, False)
        return carry

    def second_half(i, carry):
        state_step(i, True)
        return carry

    lax.fori_loop(0, n_chunks // 2, first_half, 0)
    lax.fori_loop(n_chunks // 2, n_chunks, second_half, 0)


def _rwkv(pb, mu, w0, w2, a0, a2, g2, k_k, k_a, r_k, lnx_w, lnx_b):
    b, s, n_in = pb.shape
    width = w0.shape[-1]
    pairs = width // LANES
    lo_w = n_in - 3 * width
    assert lo_w == 3 * LANES and (3 * width) % lo_w == 0 and s % (2 * RWKV_C) == 0 and s % (RWKV_C * RWKV_NB) == 0
    lo_blk = 3 * width // lo_w
    lora = w2.shape[1]
    n_chunks = s // RWKV_C
    zero = jnp.zeros((lora, width), F32)
    w2p = jnp.stack([jnp.concatenate([w2[0], zero]), jnp.concatenate([zero, w2[1]])])
    a2p = jnp.stack([jnp.concatenate([a2[0], zero]), jnp.concatenate([zero, a2[1]])])
    colv = lambda k: pl.BlockSpec((1, s, LANES), lambda i, j, k=k: (i, 0, k * pairs + j))
    mucol = lambda k: pl.BlockSpec((2, LANES), lambda i, j, k=k: (0, k * pairs + j))
    vec = pl.BlockSpec((1, LANES), lambda i, j: (0, j))
    vec2 = pl.BlockSpec((2, LANES), lambda i, j: (0, j))
    fac = pl.BlockSpec((2, LANES, LANES), lambda i, j: (0, 0, j))
    seqbuf = pltpu.VMEM((s, LANES), F32)
    n_prob = 2 * RWKV_NB
    c = RWKV_C
    staged = [pltpu.VMEM((n_prob, 2, 2 * c, LANES), BF16),
              pltpu.VMEM((n_prob, 2 * c, LANES), BF16),
              pltpu.VMEM((n_prob, 2 * c, LANES), BF16),
              pltpu.VMEM((n_prob, c, LANES), BF16),
              pltpu.VMEM((n_prob, c, LANES), F32),
              pltpu.VMEM((n_prob, 1, LANES), F32),
              pltpu.VMEM((RWKV_NB, c, LANES), BF16)]
    return pl.pallas_call(
        functools.partial(_rwkv_body, seq=s),
        grid=(b, pairs),
        in_specs=[colv(0), colv(1), colv(2),
                  pl.BlockSpec((1, s, lo_w), lambda i, j: (i, 0, lo_blk)),
                  mucol(0), mucol(1), mucol(2),
                  pl.BlockSpec((2, lo_w), lambda i, j: (0, lo_blk)),
                  vec2, fac, vec2, fac,
                  pl.BlockSpec((LANES, LANES), lambda i, j: (0, j)),
                  vec, vec, vec, vec, vec],
        out_specs=pl.BlockSpec((1, s, LANES), lambda i, j: (i, 0, j)),
        out_shape=jax.ShapeDtypeStruct((b, s, width), BF16),
        scratch_shapes=[seqbuf,
                        pltpu.VMEM((2, s, LANES), BF16), pltpu.VMEM((2, s, LANES), F32),
                        pltpu.VMEM((2, n_chunks, LANES, 3 * LANES), BF16),
                        pltpu.VMEM((2, n_chunks, LANES, LANES), F32),
                        seqbuf, seqbuf, seqbuf,
                        pltpu.VMEM((2, LANES, LANES), F32)] + staged,
        compiler_params=_cparams(("parallel", "parallel")),
        name="rwkv7",
    )(pb, pb, pb, pb, mu, mu, mu, mu, w0, w2p.astype(BF16), a0, a2p.astype(BF16), g2.astype(BF16),
      k_k.reshape(1, width), k_a.reshape(1, width), r_k.reshape(1, width),
      lnx_w.reshape(1, width), lnx_b.reshape(1, width))


def _fnet_body(u_ref, cs_ref, f_ref, o_ref, z_ref, *, seq, groups, tr):
    for g in range(groups):
        z = _mmb(u_ref[0, :, g * LANES:(g + 1) * LANES], cs_ref[...])
        z_ref[0:seq, g * LANES:(g + 1) * LANES] = z[:, :LANES].astype(BF16)
        z_ref[seq:2 * seq, g * LANES:(g + 1) * LANES] = z[:, LANES:].astype(BF16)
    scale = float(1.0 / np.sqrt(float(seq * LANES)))

    def rows(i, carry):
        t0 = pl.multiple_of(i * tr, tr)
        o_ref[0, pl.ds(t0, tr), :] = (jnp.dot(f_ref[pl.ds(t0, tr), :], z_ref[...],
                                              preferred_element_type=F32) * scale).astype(o_ref.dtype)
        return carry

    lax.fori_loop(0, seq // tr, rows, 0)


def _dft_tables(seq):
    def table(n, cols):
        ang = ((jnp.arange(n, dtype=jnp.int32)[:, None] * cols[None, :]) % n).astype(F32) * float(2.0 * np.pi / n)
        return jnp.cos(ang), jnp.sin(ang)

    step = LANES if seq % LANES == 0 else 1
    c_hi, s_hi = table(seq, jnp.arange(seq // step, dtype=jnp.int32) * step)
    c_lo, s_lo = table(seq, jnp.arange(step, dtype=jnp.int32))
    cs = (c_hi[:, :, None] * c_lo[:, None, :] - s_hi[:, :, None] * s_lo[:, None, :]).reshape(seq, seq)
    ss = (s_hi[:, :, None] * c_lo[:, None, :] + c_hi[:, :, None] * s_lo[:, None, :]).reshape(seq, seq)
    cc, sc = table(LANES, jnp.arange(LANES, dtype=jnp.int32))
    return (jnp.concatenate([cs, -ss], axis=1).astype(BF16),
            jnp.concatenate([cc, sc], axis=1).astype(BF16))


def _fnet(pc):
    b, s, w = pc.shape
    groups = w // LANES
    f_tab, c_tab = _dft_tables(s)
    tr = min(s, 512)
    return pl.pallas_call(
        functools.partial(_fnet_body, seq=s, groups=groups, tr=tr),
        grid=(b,),
        in_specs=[pl.BlockSpec((1, s, w), lambda i: (i, 0, 0)),
                  _const_spec((LANES, 2 * LANES)),
                  _const_spec((s, 2 * s))],
        out_specs=pl.BlockSpec((1, s, w), lambda i: (i, 0, 0)),
        out_shape=jax.ShapeDtypeStruct((b, s, w), BF16),
        scratch_shapes=[pltpu.VMEM((2 * s, w), BF16)],
        compiler_params=_cparams(("parallel",)),
        name="fnet",
    )(pc, c_tab, f_tab)


def _conformer_body(u_ref, w_ref, b_ref, g_ref, beta_ref, o_ref, hp_ref, *, seq, width, tr):
    zeros = jnp.zeros((CONV_PAD, width), F32)
    hp_ref[0:CONV_PAD, :] = zeros
    hp_ref[CONV_PAD + seq:CONV_PAD + seq + CONV_PAD, :] = zeros

    def glu(i, carry):
        t0 = pl.multiple_of(i * tr, tr)
        val = u_ref[0, pl.ds(t0, tr), 0:width]
        gate = u_ref[0, pl.ds(t0, tr), width:2 * width]
        hp_ref[pl.ds(CONV_PAD + t0, tr), :] = val * _sigmoid(gate)
        return carry

    lax.fori_loop(0, seq // tr, glu, 0)

    def conv(i, carry):
        t0 = pl.multiple_of(i * tr, tr)
        cols = []
        for l0 in range(0, width, LANES):
            win = hp_ref[pl.ds(t0, tr + 2 * CONV_PAD), l0:l0 + LANES]
            acc = jnp.zeros((tr, LANES), F32) + b_ref[:, l0:l0 + LANES]
            for sub in range(SUBLANES):
                rot = win if sub == 0 else pltpu.roll(win, tr + 2 * CONV_PAD - sub, 0)
                for j in range(CONV_K):
                    off = j + CONV_PAD - CONV_K // 2
                    if off % SUBLANES == sub:
                        base = off - sub
                        acc = acc + w_ref[j:j + 1, l0:l0 + LANES] * rot[base:base + tr]
            cols.append(acc)
        acc = jnp.concatenate(cols, axis=1)
        mu = jnp.mean(acc, axis=-1, keepdims=True)
        xc = acc - mu
        y = xc * lax.rsqrt(jnp.mean(xc * xc, axis=-1, keepdims=True) + LN_EPS)
        o_ref[0, pl.ds(t0, tr), :] = _silu(y * g_ref[...] + beta_ref[...]).astype(o_ref.dtype)
        return carry

    lax.fori_loop(0, seq // tr, conv, 0)


def _conformer(pd, conv_w, conv_b, ln_g, ln_b):
    b, s, w2 = pd.shape
    w = w2 // 2
    tr = 64
    vec = pl.BlockSpec((1, w), lambda i: (0, 0))
    return pl.pallas_call(
        functools.partial(_conformer_body, seq=s, width=w, tr=tr),
        grid=(b,),
        in_specs=[pl.BlockSpec((1, s, w2), lambda i: (i, 0, 0)),
                  pl.BlockSpec((CONV_K, w), lambda i: (0, 0)),
                  vec, vec, vec],
        out_specs=pl.BlockSpec((1, s, w), lambda i: (i, 0, 0)),
        out_shape=jax.ShapeDtypeStruct((b, s, w), BF16),
        scratch_shapes=[pltpu.VMEM((s + 2 * CONV_PAD, w), F32)],
        compiler_params=_cparams(("parallel",)),
        name="conformer_conv",
    )(pd, conv_w, conv_b.reshape(1, w), ln_g.reshape(1, w), ln_b.reshape(1, w))


def _mix_ffn_body(x_ref, xp_ref, xn_ref, ya_ref, yap_ref, yan_ref, yb_ref, ybp_ref, ybn_ref,
                  mgate_ref, wo_ref, g_ref, sc_ref, sh_ref, gate_ref, wup_ref, cw_ref, cb_ref,
                  wd_ref, fg_ref, o_ref, act_ref, *, seq, tm, final_norm):
    rows = tm + 2 * HALO
    dff = wd_ref.shape[0]
    with_halo = lambda p, m, n: jnp.concatenate([p[0], m[0], n[0]], axis=0)
    mix = (jnp.dot(with_halo(yap_ref, ya_ref, yan_ref), wo_ref[0], preferred_element_type=F32)
           + jnp.dot(with_halo(ybp_ref, yb_ref, ybn_ref), wo_ref[1], preferred_element_type=F32))
    xa = with_halo(xp_ref, x_ref, xn_ref) + mgate_ref[0] * mix
    x_mid = xa[HALO:HALO + tm]
    hb = _modulated_norm(xa, g_ref[...], sc_ref[0], sh_ref[0]).astype(BF16)
    pos = pl.program_id(1) * tm - HALO + lax.broadcasted_iota(jnp.int32, (rows, 1), 0)
    inside = (pos >= 0) & (pos < seq)

    for lo in range(0, dff, FF_CHUNK):
        hi = lo + FF_CHUNK
        u = jnp.dot(hb, wup_ref[:, lo:hi], preferred_element_type=F32)
        u = jnp.where(inside, u, 0.0)
        v = jnp.dot(hb[HALO:HALO + tm], wup_ref[:, dff + lo:dff + hi], preferred_element_type=F32)
        u_prev = pltpu.roll(u, 1, 0)[HALO:HALO + tm]
        u_next = pltpu.roll(u, rows - 1, 0)[HALO:HALO + tm]
        uc = (cw_ref[0:1, lo:hi] * u_prev + cw_ref[1:2, lo:hi] * u[HALO:HALO + tm]
              + cw_ref[2:3, lo:hi] * u_next + cb_ref[:, lo:hi])
        act_ref[:, lo:hi] = (_silu(uc) * v).astype(BF16)
    y = x_mid + gate_ref[0] * jnp.dot(act_ref[...], wd_ref[...], preferred_element_type=F32)
    if final_norm:
        ms = jnp.mean(y * y, axis=-1, keepdims=True)
        y = y * lax.rsqrt(ms + RMS_EPS) * fg_ref[...]
    o_ref[0] = y


def _mix_ffn(x, ya, yb, mix_gate, w_out, g, scale, shift, gate, w_up, conv_w, conv_b, w_down,
             final_g, final_norm, tm):
    b, s, d = x.shape
    dff = w_down.shape[0]
    half = ya.shape[-1]
    assert dff % FF_CHUNK == 0 and s % tm == 0 and tm % HALO == 0
    hb = tm // HALO
    last = s // HALO - 1
    tiles = lambda n: [pl.BlockSpec((1, tm, n), lambda i, j: (i, j, 0)),
                       pl.BlockSpec((1, HALO, n), lambda i, j: (i, jnp.maximum(j * hb - 1, 0), 0)),
                       pl.BlockSpec((1, HALO, n), lambda i, j: (i, jnp.minimum((j + 1) * hb, last), 0))]
    row = pl.BlockSpec((1, 1, d), lambda i, j: (i, 0, 0))
    vec = pl.BlockSpec((1, d), lambda i, j: (0, 0))
    return pl.pallas_call(
        functools.partial(_mix_ffn_body, seq=s, tm=tm, final_norm=final_norm),
        grid=(b, s // tm),
        in_specs=[*tiles(d), *tiles(half), *tiles(half),
                  row, _const_spec((2, half, d)),
                  vec, row, row, row,
                  _const_spec((d, 2 * dff)),
                  _const_spec((conv_w.shape[0], dff)), _const_spec((1, dff)),
                  _const_spec((dff, d)),
                  vec],
        out_specs=pl.BlockSpec((1, tm, d), lambda i, j: (i, j, 0)),
        out_shape=jax.ShapeDtypeStruct((b, s, d), F32),
        scratch_shapes=[pltpu.VMEM((tm, dff), BF16)],
        compiler_params=_cparams(("parallel", "parallel")),
        name="mix_ffn",
    )(x, x, x, ya, ya, ya, yb, yb, yb, mix_gate, w_out.reshape(2, half, d).astype(BF16),
      g.reshape(1, d), scale, shift, gate, w_up.astype(BF16), conv_w, conv_b.reshape(1, dff),
      w_down.astype(BF16), final_g.reshape(1, d))


def kernel(x, c, ada_w, ada_b, norm_g, final_g, ab_w_in, ab_w_out, hgrn_gamma, hgrn_norm_g, rwkv_mu, rwkv_w0, rwkv_w2, rwkv_a0, rwkv_a2, rwkv_g2, rwkv_kk, rwkv_ka, rwkv_rk, rwkv_lnx_w, rwkv_lnx_b, cd_w_in, cd_w_out, dconv_w, dconv_b, dconv_ln_g, dconv_ln_b, ffn_w_up, ffn_conv_w, ffn_conv_b, ffn_w_down):
    bsz, seq, d = x.shape
    depth = ada_w.shape[0]
    a_in = 5 * (hgrn_gamma.shape[-1])
    c_width = dconv_w.shape[-1]
    tm = min(seq, 512)

    mod = _adaln(c, ada_w.reshape(depth * 2, d, 3 * d), ada_b.reshape(depth * 2, 3 * d))
    mod = mod.reshape(depth, 2, bsz, 3, 1, d)

    for l in range(depth):
        j = l // 2
        shift, scale, gate = mod[l, 0, :, 0], mod[l, 0, :, 1], mod[l, 0, :, 2]
        if l % 2 == 0:
            pa, pb = _inproj(x, norm_g[l, 0], scale, shift, ab_w_in[j],
                             (a_in, ab_w_in.shape[-1] - a_in), (F32, F32), tm)
            y1 = _hgrn(pa, hgrn_gamma, hgrn_norm_g[j], l)
            y2 = _rwkv(pb, rwkv_mu[j], rwkv_w0[j], rwkv_w2[j], rwkv_a0[j], rwkv_a2[j], rwkv_g2[j],
                       rwkv_kk[j], rwkv_ka[j], rwkv_rk[j], rwkv_lnx_w[j], rwkv_lnx_b[j])
            w_out = ab_w_out[j]
        else:
            pc, pd = _inproj(x, norm_g[l, 0], scale, shift, cd_w_in[j],
                             (cd_w_in.shape[-1] - 2 * c_width, 2 * c_width), (BF16, F32), tm)
            y1 = _fnet(pc)
            y2 = _conformer(pd, dconv_w[j], dconv_b[j], dconv_ln_g[j], dconv_ln_b[j])
            w_out = cd_w_out[j]
        shift, scale, ffn_gate = mod[l, 1, :, 0], mod[l, 1, :, 1], mod[l, 1, :, 2]
        x = _mix_ffn(x, y1, y2, gate, w_out, norm_g[l, 1], scale, shift, ffn_gate, ffn_w_up[l],
                     ffn_conv_w[l], ffn_conv_b[l], ffn_w_down[l], final_g, l == depth - 1, tm)
    return x
```

```python
import functools

import numpy as np
import jax
import jax.numpy as jnp
from jax import lax
from jax.experimental import pallas as pl
from jax.experimental.pallas import tpu as pltpu

F32 = jnp.float32
BF16 = jnp.bfloat16
HI = lax.Precision.HIGHEST

LANES = 128
SUBLANES = 8
VMEM_LIMIT_BYTES = 56 * 1024 * 1024

RMS_EPS = 1e-6
LN_EPS = 1e-5
RWKV_GN_EPS = 64e-5
RWKV_DH = 64
CONV_K = 31
CONV_PAD = 16
HGRN_C = 64
HGRN_NB = 8
RWKV_C = 128
RWKV_NB = 4
AB_INTERLEAVE = 4
EPILOGUE_ROWS = 256
FF_CHUNK = 256
HALO = 2 * SUBLANES

NN = (((1,), (0,)), ((), ()))
NT = (((1,), (1,)), ((), ()))
TN = (((0,), (0,)), ((), ()))


def _mmf(a, b, dims=NN):
    return lax.dot_general(a, b, dims, precision=HI, preferred_element_type=F32)


def _mmb(a, b, dims=NN):
    return lax.dot_general(a.astype(BF16), b.astype(BF16), dims, preferred_element_type=F32)


def _sigmoid(x):
    return jax.nn.sigmoid(x)


def _silu(x):
    return x * jax.nn.sigmoid(x)


def _cparams(sem):
    return pltpu.CompilerParams(dimension_semantics=sem, vmem_limit_bytes=VMEM_LIMIT_BYTES)


def _const_spec(shape):
    nd = len(shape)
    return pl.BlockSpec(shape, lambda *_: (0,) * nd, pipeline_mode=pl.Buffered(1))


def _adaln_body(c_ref, w_ref, b_ref, o_ref):
    o_ref[0] = _mmf(_silu(c_ref[...]), w_ref[0]) + b_ref[0]


def _adaln(c, ada_w, ada_b):
    k, d, d3 = ada_w.shape
    b = c.shape[0]
    return pl.pallas_call(
        _adaln_body,
        grid=(k, d3 // d),
        in_specs=[pl.BlockSpec((b, d), lambda i, j: (0, 0)),
                  pl.BlockSpec((1, d, d), lambda i, j: (i, 0, j)),
                  pl.BlockSpec((1, 1, d), lambda i, j: (i, 0, j))],
        out_specs=pl.BlockSpec((1, b, d), lambda i, j: (i, 0, j)),
        out_shape=jax.ShapeDtypeStruct((k, b, d3), F32),
        compiler_params=_cparams(("parallel", "parallel")),
        name="adaln",
    )(c, ada_w, ada_b.reshape(k, 1, d3))


def _modulated_norm(x, g, scale, shift):
    ms = jnp.mean(x * x, axis=-1, keepdims=True)
    return (x * lax.rsqrt(ms + RMS_EPS) * g) * (1.0 + scale) + shift


def _inproj_body(x_ref, g_ref, sc_ref, sh_ref, w_ref, *o_refs):
    hb = _modulated_norm(x_ref[0], g_ref[...], sc_ref[0], sh_ref[0]).astype(BF16)
    off = 0
    for o_ref in o_refs:
        n = o_ref.shape[-1]
        o_ref[0] = jnp.dot(hb, w_ref[:, off:off + n], preferred_element_type=F32).astype(o_ref.dtype)
        off += n


def _inproj(x, g, scale, shift, w, splits, dtypes, tm):
    b, s, d = x.shape
    n = w.shape[1]
    assert sum(splits) == n and s % tm == 0
    row = pl.BlockSpec((1, 1, d), lambda i, j: (i, 0, 0))
    return pl.pallas_call(
        _inproj_body,
        grid=(b, s // tm),
        in_specs=[pl.BlockSpec((1, tm, d), lambda i, j: (i, j, 0)),
                  pl.BlockSpec((1, d), lambda i, j: (0, 0)),
                  row, row,
                  _const_spec((d, n))],
        out_specs=[pl.BlockSpec((1, tm, m), lambda i, j: (i, j, 0)) for m in splits],
        out_shape=[jax.ShapeDtypeStruct((b, s, m), dt) for m, dt in zip(splits, dtypes)],
        compiler_params=_cparams(("parallel", "parallel")),
        name="inproj",
    )(x, g.reshape(1, d), scale, shift, w.astype(BF16))


def _split2(x):
    hi = x.astype(BF16)
    return hi, (x - hi.astype(F32)).astype(BF16)


def _split3(x):
    hi = x.astype(BF16)
    rem = x - hi.astype(F32)
    mid = rem.astype(BF16)
    return hi, mid, (rem - mid.astype(F32)).astype(BF16)


def _mask_mm3(mask_bf16, x):
    n = x.shape[1]
    out = jnp.dot(mask_bf16, jnp.concatenate(_split3(x), axis=1), preferred_element_type=F32)
    return out[:, :n] + out[:, n:2 * n] + out[:, 2 * n:]


def _mm_mask2(x, mask_bf16):
    m = x.shape[0]
    out = jnp.dot(jnp.concatenate(_split2(x), axis=0), mask_bf16, preferred_element_type=F32)
    return out[:m] + out[m:]


def _hgrn_parts(q_ref, ff_ref, fb_ref, i_ref, g_ref, gam_ref, ng_ref, o_ref,
                of_ref, ob_ref, st_ref, *, seq, layer):
    c = HGRN_C
    nb = HGRN_NB
    n_chunks = seq // c
    rows = [gam_ref[pl.ds(i, 1), :] for i in range(gam_ref.shape[0])]
    mx = functools.reduce(jnp.maximum, rows)
    es = [jnp.exp(r - mx) for r in rows]
    lb = sum(es[:layer + 1]) / sum(es)

    r_i = lax.broadcasted_iota(jnp.int32, (c, c), 0)
    c_i = lax.broadcasted_iota(jnp.int32, (c, c), 1)
    masks = [c_i <= r_i, c_i >= r_i]
    masks_bf = [jnp.where(m, 1.0, 0.0).astype(BF16) for m in masks]
    tot_rows = [c - 1, 0]
    fraw_refs = [ff_ref, fb_ref]
    out_refs = [of_ref, ob_ref]
    st_ref[...] = jnp.zeros_like(st_ref)

    def body(i):
        probs = []
        for d in range(2):
            for u in range(nb):
                ci = i * nb + u
                probs.append((d, pl.multiple_of((ci if d == 0 else n_chunks - 1 - ci) * c, c)))
        q = [q_ref[0, pl.ds(t0, c), :] for d, t0 in probs]
        v = [i_ref[0, pl.ds(t0, c), :] for d, t0 in probs]
        f = [lb + (1.0 - lb) * _sigmoid(fraw_refs[d][0, pl.ds(t0, c), :]) for d, t0 in probs]
        cum = [_mask_mm3(masks_bf[d], jnp.log(fj)) for (d, t0), fj in zip(probs, f)]
        yield
        k = [1.0 - fj for fj in f]
        mid = [cj[c // 2:c // 2 + 1, :] for cj in cum]
        tot = [cj[tot_rows[d]:tot_rows[d] + 1, :] for (d, t0), cj in zip(probs, cum)]
        scores = [_mmb(qj * jnp.exp(cj - mj), kj * jnp.exp(mj - cj), NT)
                  for qj, kj, cj, mj in zip(q, k, cum, mid)]
        yield
        scores = [jnp.where(masks[d], sj, 0.0) for (d, t0), sj in zip(probs, scores)]
        intra = [_mmb(sj, vj) for sj, vj in zip(scores, v)]
        yield
        kv = [_mmb(vj, kj * jnp.exp(tj - cj), TN) for vj, kj, tj, cj in zip(v, k, tot, cum)]
        q_dec = [qj * jnp.exp(cj) for qj, cj in zip(q, cum)]
        dec = [jnp.exp(tj) for tj in tot]
        yield
        for d in range(2):
            st = st_ref[d]
            for j, (dj, t0) in enumerate(probs):
                if dj == d:
                    out_refs[d][pl.ds(t0, c), :] = intra[j] + _mmb(q_dec[j], st, NT)
                    st = st * dec[j] + kv[j]
            st_ref[d] = st
            yield

    te = min(seq, EPILOGUE_ROWS)

    def epilogue(i):
        t0 = pl.multiple_of(i * te, te)
        o = of_ref[pl.ds(t0, te), :] + ob_ref[pl.ds(t0, te), :]
        ms = jnp.mean(o * o, axis=-1, keepdims=True)
        y = o * lax.rsqrt(ms + RMS_EPS) * ng_ref[...]
        o_ref[0, pl.ds(t0, te), :] = (y * _silu(g_ref[0, pl.ds(t0, te), :])).astype(o_ref.dtype)

    return body, n_chunks // nb, epilogue, seq // te


def _hgrn_call(pa, gamma, norm_g):
    b, s, w5 = pa.shape
    heads = w5 // 5 // LANES
    assert s % (HGRN_C * HGRN_NB) == 0
    col = lambda k: pl.BlockSpec((1, s, LANES), lambda i, h, k=k: (i, 0, k * heads + h))
    return dict(
        heads=heads,
        in_specs=[col(0), col(1), col(2), col(3), col(4),
                  pl.BlockSpec((gamma.shape[0], LANES), lambda i, h: (0, h)),
                  pl.BlockSpec((1, LANES), lambda i, h: (0, 0))],
        args=[pa, pa, pa, pa, pa, gamma, norm_g.reshape(1, LANES)],
        out_spec=pl.BlockSpec((1, s, LANES), lambda i, h: (i, 0, h)),
        out_shape=jax.ShapeDtypeStruct((b, s, heads * LANES), BF16),
        scratch=[pltpu.VMEM((s, LANES), F32), pltpu.VMEM((s, LANES), F32),
                 pltpu.VMEM((2, LANES, LANES), F32)])


def _softplus(z):
    return jnp.maximum(z, 0.0) + jnp.log(1.0 + jnp.exp(-jnp.abs(z)))


def _tri_inverses(ns, r_i, c_i):
    size = ns[0].shape[0]
    blk = SUBLANES
    diag_blocks = r_i // blk == c_i // blk
    eye = jnp.where(r_i == c_i, 1.0, 0.0)
    n0 = [jnp.where(diag_blocks, n, 0.0).astype(BF16) for n in ns]
    t = [eye + n for n in n0]
    x = [jnp.dot(n, n, preferred_element_type=F32).astype(BF16) for n in n0]
    yield
    t = [tj + _mmb(tj, xj) for tj, xj in zip(t, x)]
    x = [jnp.dot(xj, xj, preferred_element_type=F32).astype(BF16) for xj in x]
    yield
    t = [tj + _mmb(tj, xj) for tj, xj in zip(t, x)]
    yield
    while blk < size:
        pair = (r_i // (2 * blk) == c_i // (2 * blk)) & (r_i // blk != c_i // blk)
        tb = [tj.astype(BF16) for tj in t]
        p = [jnp.dot(tj, jnp.where(pair, n, 0.0).astype(BF16), preferred_element_type=F32).astype(BF16)
             for tj, n in zip(tb, ns)]
        yield
        t = [tj + jnp.dot(pj, tbj, preferred_element_type=F32) for tj, pj, tbj in zip(t, p, tb)]
        yield
        blk *= 2
    return t


def _rwkv_parts(r_ref, k_ref, v_ref, lo_ref, mur_ref, muk_ref, muv_ref, mulo_ref,
                w0_ref, w2_ref, a0_ref, a2_ref, g2_ref, kk_ref, ka_ref, rk_ref, lw_ref, lb_ref,
                o_ref,
                gds_ref, qp_ref, y0_ref, m_ref, gm_ref, bon_ref,
                yf_ref, yb_ref, h_ref, lhs_ref, rhs_ref, end_ref, adec_ref, rdec_ref, gam_ref, vb_ref, *, seq):
    c = RWKV_C
    nb = RWKV_NB
    n_chunks = seq // c
    n_groups = n_chunks // nb

    def shifted(src, mu_ref, t0):
        cur = src[0, pl.ds(t0, c), :]
        before = src[0, pl.ds(pl.multiple_of(jnp.maximum(t0 - SUBLANES, 0), SUBLANES), SUBLANES), :]
        after = src[0, pl.ds(pl.multiple_of(jnp.minimum(t0 + c, seq - SUBLANES), SUBLANES), SUBLANES), :]
        row_before = before[SUBLANES - 1:SUBLANES, :] * (t0 > 0).astype(F32)
        row_after = after[0:1, :] * (t0 + c < seq).astype(F32)
        rows = lax.broadcasted_iota(jnp.int32, cur.shape, 0)
        prev = jnp.where(rows == 0, row_before, pltpu.roll(cur, 1, 0))
        nxt = jnp.where(rows == c - 1, row_after, pltpu.roll(cur, c - 1, 0))
        return cur + mu_ref[0:1, :] * (prev - cur) + mu_ref[1:2, :] * (nxt - cur)

    r_i = lax.broadcasted_iota(jnp.int32, (c, c), 0)
    c_i = lax.broadcasted_iota(jnp.int32, (c, c), 1)
    eye = r_i == c_i
    same_head = r_i // RWKV_DH == c_i // RWKV_DH
    head_sum = jnp.where(same_head, 1.0, 0.0).astype(BF16)
    incl = [c_i <= r_i, c_i >= r_i]
    incl_bf = [jnp.where(m, 1.0, 0.0).astype(BF16) for m in incl]
    r_q = lax.broadcasted_iota(jnp.int32, (2 * c, 2 * c), 0)
    c_q = lax.broadcasted_iota(jnp.int32, (2 * c, 2 * c), 1)
    r_l, c_l = r_q % c, c_q % c
    diag_r = (r_q >= c) & (r_l == c_l)
    quad = [(c_l < r_l) | diag_r, (c_l > r_l) | diag_r]
    tot_rows = [c - 1, 0]
    lane = lax.broadcasted_iota(jnp.int32, (1, LANES), 1)
    head_lanes = [lane < RWKV_DH, lane >= RWKV_DH]
    lane2 = lax.broadcasted_iota(jnp.int32, (1, 2 * LANES), 1)
    first_head2 = (lane2 % LANES) < RWKV_DH
    zeros = jnp.zeros((c, LANES), F32)

    def prep(grp):
        for u in range(nb):
            ci = grp * nb + u
            t0 = pl.multiple_of(ci * c, c)
            r = shifted(r_ref, mur_ref, t0)
            k = shifted(k_ref, muk_ref, t0)
            v = shifted(v_ref, muv_ref, t0)
            lo = shifted(lo_ref, mulo_ref, t0)
            gds_ref[pl.ds(t0, c), :] = lo[:, 2 * LANES:3 * LANES]
            kk = k * kk_ref[...]
            kk = kk * lax.rsqrt(_mm_mask2(kk * kk, head_sum) + 1e-12)
            tanh_w = jnp.tanh(lo[:, 0:LANES])
            bonus = zeros
            for d in range(2):
                w = w0_ref[d:d + 1, :] + _mmb(tanh_w, w2_ref[d])
                wlog = -jnp.exp(-_softplus(-w) - 0.5)
                a = _sigmoid(a0_ref[d:d + 1, :] + _mmb(lo[:, LANES:2 * LANES], a2_ref[d]))
                kd = k * (1.0 + (a - 1.0) * ka_ref[...])
                bonus = bonus + _mm_mask2(r * kd * rk_ref[...], head_sum) * v
                cum = _mask_mm3(incl_bf[d], wlog)
                mid = cum[c // 2:c // 2 + 1, :]
                tot = cum[tot_rows[d]:tot_rows[d] + 1, :]
                e1 = jnp.exp(cum - mid)
                e2 = jnp.exp(mid - cum)
                em = jnp.exp(mid)
                et = jnp.exp(tot - mid)
                a_rel = -kk * e1 * jnp.exp(-wlog)
                r_rel = r * e1
                b_inv = kk * a * e2
                k_inv = kd * e2
                j = 2 * u + d
                for h, hl in enumerate(head_lanes):
                    lhs_ref[j, h] = jnp.concatenate([jnp.where(hl, a_rel, 0.0), jnp.where(hl, r_rel, 0.0)],
                                                    axis=0).astype(BF16)
                rhs_ref[j] = jnp.concatenate([b_inv, k_inv], axis=0).astype(BF16)
                end_ref[j] = jnp.concatenate([b_inv * et, k_inv * et], axis=0).astype(BF16)
                adec_ref[j] = (a_rel * em).astype(BF16)
                rdec_ref[j] = r_rel * em
                gam_ref[j] = jnp.exp(tot)
            vb_ref[u] = v.astype(BF16)
            bon_ref[pl.ds(t0, c), :] = bonus
            yield

    def chain(grp):
        probs =[dict(ci=grp * nb + u, t0=pl.multiple_of((grp * nb + u) * c, c), d=d, u=u, j=2 * u + d)
                 for u in range(nb) for d in range(2)]
        chains = [(p, h) for p in probs for h in range(2)]
        blocks = [lax.dot_general(lhs_ref[p["j"], h], rhs_ref[p["j"]], NT, preferred_element_type=F32)
                  for p, h in chains]
        yield
        blocks = [jnp.where(quad[p["d"]], bj, 0.0) for bj, (p, h) in zip(blocks, chains)]
        a_ab = [bj[:c, :c] for bj in blocks]
        a_ak = [bj[:c, c:].astype(BF16) for bj in blocks]
        a_r = [bj[c:, :].astype(BF16) for bj in blocks]
        av = [jnp.dot(aj, vb_ref[p["u"]], preferred_element_type=F32) for aj, (p, h) in zip(a_ak, chains)]
        yield
        t = yield from _tri_inverses(a_ab, r_i, c_i)
        wu = [jnp.dot(tj.astype(BF16), jnp.concatenate([adec_ref[p["j"]], avj.astype(BF16)], axis=1),
                      preferred_element_type=F32) for tj, avj, (p, h) in zip(t, av, chains)]
        yield
        lower = [jnp.concatenate([zeros.astype(BF16), vb_ref[p["u"]]], axis=1) for p in probs]
        qy = [jnp.dot(arj, jnp.concatenate([wuj.astype(BF16), lower[j // 2]], axis=0),
                      preferred_element_type=F32) for j, (arj, wuj) in enumerate(zip(a_r, wu))]
        yield
        for j, p in enumerate(probs):
            d, ci, t0 = p["d"], p["ci"], p["t0"]
            wu_p = jnp.where(first_head2, wu[2 * j], wu[2 * j + 1])
            qy_p = jnp.where(first_head2, qy[2 * j], qy[2 * j + 1])
            qp_ref[d, pl.ds(t0, c), :] = (rdec_ref[j] + qy_p[:, :c]).astype(BF16)
            y0_ref[d, pl.ds(t0, c), :] = qy_p[:, c:]
            mg = lax.dot_general(end_ref[j], jnp.concatenate([wu_p.astype(BF16), lower[j]], axis=0),
                                 TN, preferred_element_type=F32)
            m = jnp.where(same_head, mg[:, :c], 0.0) + jnp.where(eye, gam_ref[j], 0.0)
            m_hi, m_lo = _split2(m)
            m_ref[d, ci] = jnp.concatenate([m_hi, m_hi, m_lo], axis=1)
            gm_ref[d, ci] = jnp.where(same_head, mg[:, c:], 0.0)
            if j % 2:
                yield

    def phase_a(grp):
        yield from prep(grp)
        yield from chain(grp)

    h_ref[...] = jnp.zeros_like(h_ref)
    inv_dh = 1.0 / RWKV_DH

    def finish(ys, t0s):
        yc = [y - _mm_mask2(y, head_sum) * inv_dh for y in ys]
        var = [_mm_mask2(x * x, head_sum) * inv_dh for x in yc]
        gate = [_mmb(_sigmoid(gds_ref[pl.ds(t0, c), :]), g2_ref[...]) for t0 in t0s]
        for x, vr, gt, t0 in zip(yc, var, gate, t0s):
            yn = x * lax.rsqrt(vr + RWKV_GN_EPS) * lw_ref[...] + lb_ref[...]
            o_ref[0, pl.ds(t0, c), :] = ((yn + bon_ref[pl.ds(t0, c), :]) * gt).astype(o_ref.dtype)

    def state_step(i, second_half):
        cis = [i, n_chunks - 1 - i]
        t0s = [pl.multiple_of(ci * c, c) for ci in cis]
        hs = [_split2(h_ref[d]) for d in range(2)]
        y = [jnp.dot(qp_ref[d, pl.ds(t0s[d], c), :], hs[d][0], preferred_element_type=F32)
             + y0_ref[d, pl.ds(t0s[d], c), :] for d in range(2)]
        hn = [jnp.dot(m_ref[d, cis[d]], jnp.concatenate([hs[d][0], hs[d][1], hs[d][0]], axis=0),
                      preferred_element_type=F32) + gm_ref[d, cis[d]] for d in range(2)]
        for d in range(2):
            h_ref[d] = hn[d]
        if second_half:
            finish([y[0] + yb_ref[pl.ds(t0s[0], c), :], yf_ref[pl.ds(t0s[1], c), :] + y[1]], t0s)
        else:
            yf_ref[pl.ds(t0s[0], c), :] = y[0]
            yb_ref[pl.ds(t0s[1], c), :] = y[1]

    return phase_a, n_groups, state_step, n_chunks


def _rwkv_call(pb, mu, w0, w2, a0, a2, g2, k_k, k_a, r_k, lnx_w, lnx_b):
    b, s, n_in = pb.shape
    width = w0.shape[-1]
    pairs = width // LANES
    lo_w = n_in - 3 * width
    assert lo_w == 3 * LANES and (3 * width) % lo_w == 0 and s % (2 * RWKV_C) == 0 and s % (RWKV_C * RWKV_NB) == 0
    lo_blk = 3 * width // lo_w
    lora = w2.shape[1]
    n_chunks = s // RWKV_C
    zero = jnp.zeros((lora, width), F32)
    w2p = jnp.stack([jnp.concatenate([w2[0], zero]), jnp.concatenate([zero, w2[1]])])
    a2p = jnp.stack([jnp.concatenate([a2[0], zero]), jnp.concatenate([zero, a2[1]])])
    colv = lambda k: pl.BlockSpec((1, s, LANES), lambda i, j, k=k: (i, 0, k * pairs + j))
    mucol = lambda k: pl.BlockSpec((2, LANES), lambda i, j, k=k: (0, k * pairs + j))
    vec = pl.BlockSpec((1, LANES), lambda i, j: (0, j))
    vec2 = pl.BlockSpec((2, LANES), lambda i, j: (0, j))
    fac = pl.BlockSpec((2, LANES, LANES), lambda i, j: (0, 0, j))
    seqbuf = pltpu.VMEM((s, LANES), F32)
    n_prob = 2 * RWKV_NB
    c = RWKV_C
    staged = [pltpu.VMEM((n_prob, 2, 2 * c, LANES), BF16),
              pltpu.VMEM((n_prob, 2 * c, LANES), BF16),
              pltpu.VMEM((n_prob, 2 * c, LANES), BF16),
              pltpu.VMEM((n_prob, c, LANES), BF16),
              pltpu.VMEM((n_prob, c, LANES), F32),
              pltpu.VMEM((n_prob, 1, LANES), F32),
              pltpu.VMEM((RWKV_NB, c, LANES), BF16)]
    return dict(
        heads=pairs,
        in_specs=[colv(0), colv(1), colv(2),
                  pl.BlockSpec((1, s, lo_w), lambda i, j: (i, 0, lo_blk)),
                  mucol(0), mucol(1), mucol(2),
                  pl.BlockSpec((2, lo_w), lambda i, j: (0, lo_blk)),
                  vec2, fac, vec2, fac,
                  pl.BlockSpec((LANES, LANES), lambda i, j: (0, j)),
                  vec, vec, vec, vec, vec],
        args=[pb, pb, pb, pb, mu, mu, mu, mu, w0, w2p.astype(BF16), a0, a2p.astype(BF16), g2.astype(BF16),
              k_k.reshape(1, width), k_a.reshape(1, width), r_k.reshape(1, width),
              lnx_w.reshape(1, width), lnx_b.reshape(1, width)],
        out_spec=pl.BlockSpec((1, s, LANES), lambda i, j: (i, 0, j)),
        out_shape=jax.ShapeDtypeStruct((b, s, width), BF16),
        scratch=[seqbuf,
                 pltpu.VMEM((2, s, LANES), BF16), pltpu.VMEM((2, s, LANES), F32),
                 pltpu.VMEM((2, n_chunks, LANES, 3 * LANES), BF16),
                 pltpu.VMEM((2, n_chunks, LANES, LANES), F32),
                 seqbuf, seqbuf, seqbuf,
                 pltpu.VMEM((2, LANES, LANES), F32)] + staged)


def _ab_body(*refs, n_in, n_scratch, seq, layer):
    (h_in, r_in), (h_scr, r_scr) = n_in, n_scratch
    ins, (ya_ref, yb_ref), scr = refs[:h_in + r_in], refs[h_in + r_in:h_in + r_in + 2], refs[h_in + r_in + 2:]
    assert len(scr) == h_scr + r_scr
    h_step, h_steps, h_epilogue, h_epilogues = _hgrn_parts(*ins[:h_in], ya_ref, *scr[:h_scr], seq=seq, layer=layer)
    r_group, r_groups, r_state, r_states = _rwkv_parts(*ins[h_in:], yb_ref, *scr[h_scr:], seq=seq)
    assert h_steps == r_groups and 2 * h_epilogues == r_states

    def main(i, carry):
        side = h_step(i)
        for n, _ in enumerate(r_group(i)):
            if n % AB_INTERLEAVE == AB_INTERLEAVE - 1:
                next(side, None)
        for _ in side:
            pass
        return carry

    def first_half(i, carry):
        r_state(i, False)
        h_epilogue(i)
        return carry

    def second_half(i, carry):
        r_state(i, True)
        return carry

    lax.fori_loop(0, r_groups, main, 0)
    lax.fori_loop(0, r_states // 2, first_half, 0)
    lax.fori_loop(r_states // 2, r_states, second_half, 0, unroll=2)


def _ab_mixers(pa, pb, layer, gamma, norm_g, *rwkv_params):
    hg, rw = _hgrn_call(pa, gamma, norm_g), _rwkv_call(pb, *rwkv_params)
    assert hg["heads"] == rw["heads"]
    b, s, _ = pa.shape
    return pl.pallas_call(
        functools.partial(_ab_body, n_in=(len(hg["in_specs"]), len(rw["in_specs"])),
                          n_scratch=(len(hg["scratch"]), len(rw["scratch"])), seq=s, layer=layer),
        grid=(b, hg["heads"]),
        in_specs=hg["in_specs"] + rw["in_specs"],
        out_specs=[hg["out_spec"], rw["out_spec"]],
        out_shape=[hg["out_shape"], rw["out_shape"]],
        scratch_shapes=hg["scratch"] + rw["scratch"],
        compiler_params=_cparams(("parallel", "parallel")),
        name="hgrn2_rwkv7",
    )(*hg["args"], *rw["args"])


def _fnet_body(u_ref, cs_ref, f_ref, o_ref, z_ref, *, seq, groups, tr):
    for g in range(groups):
        z = _mmb(u_ref[0, :, g * LANES:(g + 1) * LANES], cs_ref[...])
        z_ref[0:seq, g * LANES:(g + 1) * LANES] = z[:, :LANES].astype(BF16)
        z_ref[seq:2 * seq, g * LANES:(g + 1) * LANES] = z[:, LANES:].astype(BF16)
    scale = float(1.0 / np.sqrt(float(seq * LANES)))

    def rows(i, carry):
        t0 = pl.multiple_of(i * tr, tr)
        o_ref[0, pl.ds(t0, tr), :] = (jnp.dot(f_ref[pl.ds(t0, tr), :], z_ref[...],
                                              preferred_element_type=F32) * scale).astype(o_ref.dtype)
        return carry

    lax.fori_loop(0, seq // tr, rows, 0)


def _dft_tables(seq):
    def table(n, cols):
        ang = ((jnp.arange(n, dtype=jnp.int32)[:, None] * cols[None, :]) % n).astype(F32) * float(2.0 * np.pi / n)
        return jnp.cos(ang), jnp.sin(ang)

    step = LANES if seq % LANES == 0 else 1
    c_hi, s_hi = table(seq, jnp.arange(seq // step, dtype=jnp.int32) * step)
    c_lo, s_lo = table(seq, jnp.arange(step, dtype=jnp.int32))
    cs = (c_hi[:, :, None] * c_lo[:, None, :] - s_hi[:, :, None] * s_lo[:, None, :]).reshape(seq, seq)
    ss = (s_hi[:, :, None] * c_lo[:, None, :] + c_hi[:, :, None] * s_lo[:, None, :]).reshape(seq, seq)
    cc, sc = table(LANES, jnp.arange(LANES, dtype=jnp.int32))
    return (jnp.concatenate([cs, -ss], axis=1).astype(BF16),
            jnp.concatenate([cc, sc], axis=1).astype(BF16))


def _fnet(pc):
    b, s, w = pc.shape
    groups = w // LANES
    f_tab, c_tab = _dft_tables(s)
    tr = min(s, 512)
    return pl.pallas_call(
        functools.partial(_fnet_body, seq=s, groups=groups, tr=tr),
        grid=(b,),
        in_specs=[pl.BlockSpec((1, s, w), lambda i: (i, 0, 0)),
                  _const_spec((LANES, 2 * LANES)),
                  _const_spec((s, 2 * s))],
        out_specs=pl.BlockSpec((1, s, w), lambda i: (i, 0, 0)),
        out_shape=jax.ShapeDtypeStruct((b, s, w), BF16),
        scratch_shapes=[pltpu.VMEM((2 * s, w), BF16)],
        compiler_params=_cparams(("parallel",)),
        name="fnet",
    )(pc, c_tab, f_tab)


def _conformer_body(u_ref, w_ref, b_ref, g_ref, beta_ref, o_ref, hp_ref, *, seq, width, tr):
    zeros = jnp.zeros((CONV_PAD, width), F32)
    hp_ref[0:CONV_PAD, :] = zeros
    hp_ref[CONV_PAD + seq:CONV_PAD + seq + CONV_PAD, :] = zeros

    def glu(i, carry):
        t0 = pl.multiple_of(i * tr, tr)
        val = u_ref[0, pl.ds(t0, tr), 0:width]
        gate = u_ref[0, pl.ds(t0, tr), width:2 * width]
        hp_ref[pl.ds(CONV_PAD + t0, tr), :] = val * _sigmoid(gate)
        return carry

    lax.fori_loop(0, seq // tr, glu, 0)

    def conv(i, carry):
        t0 = pl.multiple_of(i * tr, tr)
        cols = []
        for l0 in range(0, width, LANES):
            win = hp_ref[pl.ds(t0, tr + 2 * CONV_PAD), l0:l0 + LANES]
            acc = jnp.zeros((tr, LANES), F32) + b_ref[:, l0:l0 + LANES]
            for sub in range(SUBLANES):
                rot = win if sub == 0 else pltpu.roll(win, tr + 2 * CONV_PAD - sub, 0)
                for j in range(CONV_K):
                    off = j + CONV_PAD - CONV_K // 2
                    if off % SUBLANES == sub:
                        base = off - sub
                        acc = acc + w_ref[j:j + 1, l0:l0 + LANES] * rot[base:base + tr]
            cols.append(acc)
        acc = jnp.concatenate(cols, axis=1)
        mu = jnp.mean(acc, axis=-1, keepdims=True)
        xc = acc - mu
        y = xc * lax.rsqrt(jnp.mean(xc * xc, axis=-1, keepdims=True) + LN_EPS)
        o_ref[0, pl.ds(t0, tr), :] = _silu(y * g_ref[...] + beta_ref[...]).astype(o_ref.dtype)
        return carry

    lax.fori_loop(0, seq // tr, conv, 0)


def _conformer(pd, conv_w, conv_b, ln_g, ln_b):
    b, s, w2 = pd.shape
    w = w2 // 2
    tr = 64
    vec = pl.BlockSpec((1, w), lambda i: (0, 0))
    return pl.pallas_call(
        functools.partial(_conformer_body, seq=s, width=w, tr=tr),
        grid=(b,),
        in_specs=[pl.BlockSpec((1, s, w2), lambda i: (i, 0, 0)),
                  pl.BlockSpec((CONV_K, w), lambda i: (0, 0)),
                  vec, vec, vec],
        out_specs=pl.BlockSpec((1, s, w), lambda i: (i, 0, 0)),
        out_shape=jax.ShapeDtypeStruct((b, s, w), BF16),
        scratch_shapes=[pltpu.VMEM((s + 2 * CONV_PAD, w), F32)],
        compiler_params=_cparams(("parallel",)),
        name="conformer_conv",
    )(pd, conv_w, conv_b.reshape(1, w), ln_g.reshape(1, w), ln_b.reshape(1, w))


def _mix_ffn_body(x_ref, xp_ref, xn_ref, ya_ref, yap_ref, yan_ref, yb_ref, ybp_ref, ybn_ref,
                  mgate_ref, wo_ref, g_ref, sc_ref, sh_ref, gate_ref, wup_ref, cw_ref, cb_ref,
                  wd_ref, fg_ref, o_ref, act_ref, *, seq, tm, final_norm):
    rows = tm + 2 * HALO
    dff = wd_ref.shape[0]
    with_halo = lambda p, m, n: jnp.concatenate([p[0], m[0], n[0]], axis=0)
    mix = (jnp.dot(with_halo(yap_ref, ya_ref, yan_ref), wo_ref[0], preferred_element_type=F32)
           + jnp.dot(with_halo(ybp_ref, yb_ref, ybn_ref), wo_ref[1], preferred_element_type=F32))
    xa = with_halo(xp_ref, x_ref, xn_ref) + mgate_ref[0] * mix
    x_mid = xa[HALO:HALO + tm]
    hb = _modulated_norm(xa, g_ref[...], sc_ref[0], sh_ref[0]).astype(BF16)
    pos = pl.program_id(1) * tm - HALO + lax.broadcasted_iota(jnp.int32, (rows, 1), 0)
    inside = (pos >= 0) & (pos < seq)

    for lo in range(0, dff, FF_CHUNK):
        hi = lo + FF_CHUNK
        u = jnp.dot(hb, wup_ref[:, lo:hi], preferred_element_type=F32)
        u = jnp.where(inside, u, 0.0)
        v = jnp.dot(hb[HALO:HALO + tm], wup_ref[:, dff + lo:dff + hi], preferred_element_type=F32)
        u_prev = pltpu.roll(u, 1, 0)[HALO:HALO + tm]
        u_next = pltpu.roll(u, rows - 1, 0)[HALO:HALO + tm]
        uc = (cw_ref[0:1, lo:hi] * u_prev + cw_ref[1:2, lo:hi] * u[HALO:HALO + tm]
              + cw_ref[2:3, lo:hi] * u_next + cb_ref[:, lo:hi])
        act_ref[:, lo:hi] = (_silu(uc) * v).astype(BF16)
    y = x_mid + gate_ref[0] * jnp.dot(act_ref[...], wd_ref[...], preferred_element_type=F32)
    if final_norm:
        ms = jnp.mean(y * y, axis=-1, keepdims=True)
        y = y * lax.rsqrt(ms + RMS_EPS) * fg_ref[...]
    o_ref[0] = y


def _mix_ffn(x, ya, yb, mix_gate, w_out, g, scale, shift, gate, w_up, conv_w, conv_b, w_down,
             final_g, final_norm, tm):
    b, s, d = x.shape
    dff = w_down.shape[0]
    half = ya.shape[-1]
    assert dff % FF_CHUNK == 0 and s % tm == 0 and tm % HALO == 0
    hb = tm // HALO
    last = s // HALO - 1
    tiles = lambda n: [pl.BlockSpec((1, tm, n), lambda i, j: (i, j, 0)),
                       pl.BlockSpec((1, HALO, n), lambda i, j: (i, jnp.maximum(j * hb - 1, 0), 0)),
                       pl.BlockSpec((1, HALO, n), lambda i, j: (i, jnp.minimum((j + 1) * hb, last), 0))]
    row = pl.BlockSpec((1, 1, d), lambda i, j: (i, 0, 0))
    vec = pl.BlockSpec((1, d), lambda i, j: (0, 0))
    return pl.pallas_call(
        functools.partial(_mix_ffn_body, seq=s, tm=tm, final_norm=final_norm),
        grid=(b, s // tm),
        in_specs=[*tiles(d), *tiles(half), *tiles(half),
                  row, _const_spec((2, half, d)),
                  vec, row, row, row,
                  _const_spec((d, 2 * dff)),
                  _const_spec((conv_w.shape[0], dff)), _const_spec((1, dff)),
                  _const_spec((dff, d)),
                  vec],
        out_specs=pl.BlockSpec((1, tm, d), lambda i, j: (i, j, 0)),
        out_shape=jax.ShapeDtypeStruct((b, s, d), F32),
        scratch_shapes=[pltpu.VMEM((tm, dff), BF16)],
        compiler_params=_cparams(("parallel", "parallel")),
        name="mix_ffn",
    )(x, x, x, ya, ya, ya, yb, yb, yb, mix_gate, w_out.reshape(2, half, d).astype(BF16),
      g.reshape(1, d), scale, shift, gate, w_up.astype(BF16), conv_w, conv_b.reshape(1, dff),
      w_down.astype(BF16), final_g.reshape(1, d))


def kernel(x, c, ada_w, ada_b, norm_g, final_g, ab_w_in, ab_w_out, hgrn_gamma, hgrn_norm_g, rwkv_mu, rwkv_w0, rwkv_w2, rwkv_a0, rwkv_a2, rwkv_g2, rwkv_kk, rwkv_ka, rwkv_rk, rwkv_lnx_w, rwkv_lnx_b, cd_w_in, cd_w_out, dconv_w, dconv_b, dconv_ln_g, dconv_ln_b, ffn_w_up, ffn_conv_w, ffn_conv_b, ffn_w_down):
    bsz, seq, d = x.shape
    depth = ada_w.shape[0]
    a_in = 5 * (hgrn_gamma.shape[-1])
    c_width = dconv_w.shape[-1]
    tm = min(seq, 512)

    mod = _adaln(c, ada_w.reshape(depth * 2, d, 3 * d), ada_b.reshape(depth * 2, 3 * d))
    mod = mod.reshape(depth, 2, bsz, 3, 1, d)

    for l in range(depth):
        j = l // 2
        shift, scale, gate = mod[l, 0, :, 0], mod[l, 0, :, 1], mod[l, 0, :, 2]
        if l % 2 == 0:
            pa, pb = _inproj(x, norm_g[l, 0], scale, shift, ab_w_in[j],
                             (a_in, ab_w_in.shape[-1] - a_in), (F32, F32), tm)
            y1, y2 = _ab_mixers(pa, pb, l, hgrn_gamma, hgrn_norm_g[j],
                                rwkv_mu[j], rwkv_w0[j], rwkv_w2[j], rwkv_a0[j], rwkv_a2[j], rwkv_g2[j],
                                rwkv_kk[j], rwkv_ka[j], rwkv_rk[j], rwkv_lnx_w[j], rwkv_lnx_b[j])
            w_out = ab_w_out[j]
        else:
            pc, pd = _inproj(x, norm_g[l, 0], scale, shift, cd_w_in[j],
                             (cd_w_in.shape[-1] - 2 * c_width, 2 * c_width), (BF16, F32), tm)
            y1 = _fnet(pc)
            y2 = _conformer(pd, dconv_w[j], dconv_b[j], dconv_ln_g[j], dconv_ln_b[j])
            w_out = cd_w_out[j]
        shift, scale, ffn_gate = mod[l, 1, :, 0], mod[l, 1, :, 1], mod[l, 1, :, 2]
        x = _mix_ffn(x, y1, y2, gate, w_out, norm_g[l, 1], scale, shift, ffn_gate, ffn_w_up[l],
                     ffn_conv_w[l], ffn_conv_b[l], ffn_w_down[l], final_g, l == depth - 1, tm)
    return x
```

```python
import functools

import numpy as np
import jax
import jax.numpy as jnp
from jax import lax
from jax.experimental import pallas as pl
from jax.experimental.pallas import tpu as pltpu

F32 = jnp.float32
BF16 = jnp.bfloat16
HI = lax.Precision.HIGHEST

LANES = 128
SUBLANES = 8
VMEM_LIMIT_BYTES = 56 * 1024 * 1024

RMS_EPS = 1e-6
LN_EPS = 1e-5
RWKV_GN_EPS = 64e-5
RWKV_DH = 64
CONV_K = 31
CONV_PAD = 16
HGRN_C = 64
HGRN_NB = 8
RWKV_C = 128
RWKV_NB = 4
AB_INTERLEAVE = 4
EPILOGUE_ROWS = 256
FF_CHUNK = 256
HALO = 2 * SUBLANES

NN = (((1,), (0,)), ((), ()))
NT = (((1,), (1,)), ((), ()))
TN = (((0,), (0,)), ((), ()))


def _mmf(a, b, dims=NN):
    return lax.dot_general(a, b, dims, precision=HI, preferred_element_type=F32)


def _mmb(a, b, dims=NN):
    return lax.dot_general(a.astype(BF16), b.astype(BF16), dims, preferred_element_type=F32)


def _sigmoid(x):
    return jax.nn.sigmoid(x)


def _silu(x):
    return x * jax.nn.sigmoid(x)


def _cparams(sem):
    return pltpu.CompilerParams(dimension_semantics=sem, vmem_limit_bytes=VMEM_LIMIT_BYTES)


def _const_spec(shape):
    nd = len(shape)
    return pl.BlockSpec(shape, lambda *_: (0,) * nd, pipeline_mode=pl.Buffered(1))


def _adaln_body(c_ref, w_ref, b_ref, o_ref):
    o_ref[0] = _mmf(_silu(c_ref[...]), w_ref[0]) + b_ref[0]


def _adaln(c, ada_w, ada_b):
    k, d, d3 = ada_w.shape
    b = c.shape[0]
    return pl.pallas_call(
        _adaln_body,
        grid=(k, d3 // d),
        in_specs=[pl.BlockSpec((b, d), lambda i, j: (0, 0)),
                  pl.BlockSpec((1, d, d), lambda i, j: (i, 0, j)),
                  pl.BlockSpec((1, 1, d), lambda i, j: (i, 0, j))],
        out_specs=pl.BlockSpec((1, b, d), lambda i, j: (i, 0, j)),
        out_shape=jax.ShapeDtypeStruct((k, b, d3), F32),
        compiler_params=_cparams(("parallel", "parallel")),
        name="adaln",
    )(c, ada_w, ada_b.reshape(k, 1, d3))


def _modulated_norm(x, g, scale, shift):
    ms = jnp.mean(x * x, axis=-1, keepdims=True)
    return (x * lax.rsqrt(ms + RMS_EPS) * g) * (1.0 + scale) + shift


def _inproj_body(x_ref, g_ref, sc_ref, sh_ref, w_ref, *o_refs):
    hb = _modulated_norm(x_ref[0], g_ref[...], sc_ref[0], sh_ref[0]).astype(BF16)
    off = 0
    for o_ref in o_refs:
        n = o_ref.shape[-1]
        o_ref[0] = jnp.dot(hb, w_ref[:, off:off + n], preferred_element_type=F32).astype(o_ref.dtype)
        off += n


def _inproj(x, g, scale, shift, w, splits, dtypes, tm):
    b, s, d = x.shape
    n = w.shape[1]
    assert sum(splits) == n and s % tm == 0
    row = pl.BlockSpec((1, 1, d), lambda i, j: (i, 0, 0))
    return pl.pallas_call(
        _inproj_body,
        grid=(b, s // tm),
        in_specs=[pl.BlockSpec((1, tm, d), lambda i, j: (i, j, 0)),
                  pl.BlockSpec((1, d), lambda i, j: (0, 0)),
                  row, row,
                  _const_spec((d, n))],
        out_specs=[pl.BlockSpec((1, tm, m), lambda i, j: (i, j, 0)) for m in splits],
        out_shape=[jax.ShapeDtypeStruct((b, s, m), dt) for m, dt in zip(splits, dtypes)],
        compiler_params=_cparams(("parallel", "parallel")),
        name="inproj",
    )(x, g.reshape(1, d), scale, shift, w.astype(BF16))


def _split2(x):
    hi = x.astype(BF16)
    return hi, (x - hi.astype(F32)).astype(BF16)


def _mask_mm2(mask_bf16, x):
    n = x.shape[1]
    out = jnp.dot(mask_bf16, jnp.concatenate(_split2(x), axis=1), preferred_element_type=F32)
    return out[:, :n] + out[:, n:]


def _mm_mask(x, mask_bf16):
    return jnp.dot(x.astype(BF16), mask_bf16, preferred_element_type=F32)


def _hgrn_parts(q_ref, ff_ref, fb_ref, i_ref, g_ref, gam_ref, ng_ref, o_ref,
                of_ref, ob_ref, st_ref, *, seq, layer):
    c = HGRN_C
    nb = HGRN_NB
    n_chunks = seq // c
    rows = [gam_ref[pl.ds(i, 1), :] for i in range(gam_ref.shape[0])]
    mx = functools.reduce(jnp.maximum, rows)
    es = [jnp.exp(r - mx) for r in rows]
    lb = sum(es[:layer + 1]) / sum(es)

    r_i = lax.broadcasted_iota(jnp.int32, (c, c), 0)
    c_i = lax.broadcasted_iota(jnp.int32, (c, c), 1)
    masks = [c_i <= r_i, c_i >= r_i]
    masks_bf = [jnp.where(m, 1.0, 0.0).astype(BF16) for m in masks]
    tot_rows = [c - 1, 0]
    fraw_refs = [ff_ref, fb_ref]
    out_refs = [of_ref, ob_ref]
    st_ref[...] = jnp.zeros_like(st_ref)

    def body(i):
        probs = []
        for d in range(2):
            for u in range(nb):
                ci = i * nb + u
                probs.append((d, pl.multiple_of((ci if d == 0 else n_chunks - 1 - ci) * c, c)))
        q = [q_ref[0, pl.ds(t0, c), :] for d, t0 in probs]
        v = [i_ref[0, pl.ds(t0, c), :] for d, t0 in probs]
        f = [lb + (1.0 - lb) * _sigmoid(fraw_refs[d][0, pl.ds(t0, c), :]) for d, t0 in probs]
        cum = [_mask_mm2(masks_bf[d], jnp.log(fj)) for (d, t0), fj in zip(probs, f)]
        yield
        k = [1.0 - fj for fj in f]
        mid = [cj[c // 2:c // 2 + 1, :] for cj in cum]
        tot = [cj[tot_rows[d]:tot_rows[d] + 1, :] for (d, t0), cj in zip(probs, cum)]
        scores = [_mmb(qj * jnp.exp(cj - mj), kj * jnp.exp(mj - cj), NT)
                  for qj, kj, cj, mj in zip(q, k, cum, mid)]
        yield
        scores = [jnp.where(masks[d], sj, 0.0) for (d, t0), sj in zip(probs, scores)]
        intra = [_mmb(sj, vj) for sj, vj in zip(scores, v)]
        yield
        kv = [_mmb(vj, kj * jnp.exp(tj - cj), TN) for vj, kj, tj, cj in zip(v, k, tot, cum)]
        q_dec = [qj * jnp.exp(cj) for qj, cj in zip(q, cum)]
        dec = [jnp.exp(tj) for tj in tot]
        yield
        for d in range(2):
            st = st_ref[d]
            for j, (dj, t0) in enumerate(probs):
                if dj == d:
                    out_refs[d][pl.ds(t0, c), :] = intra[j] + _mmb(q_dec[j], st, NT)
                    st = st * dec[j] + kv[j]
            st_ref[d] = st
            yield

    te = min(seq, EPILOGUE_ROWS)

    def epilogue(i):
        t0 = pl.multiple_of(i * te, te)
        o = of_ref[pl.ds(t0, te), :] + ob_ref[pl.ds(t0, te), :]
        ms = jnp.mean(o * o, axis=-1, keepdims=True)
        y = o * lax.rsqrt(ms + RMS_EPS) * ng_ref[...]
        o_ref[0, pl.ds(t0, te), :] = (y * _silu(g_ref[0, pl.ds(t0, te), :])).astype(o_ref.dtype)

    return body, n_chunks // nb, epilogue, seq // te


def _hgrn_call(pa, gamma, norm_g):
    b, s, w5 = pa.shape
    heads = w5 // 5 // LANES
    assert s % (HGRN_C * HGRN_NB) == 0
    col = lambda k: pl.BlockSpec((1, s, LANES), lambda i, h, k=k: (i, 0, k * heads + h))
    return dict(
        heads=heads,
        in_specs=[col(0), col(1), col(2), col(3), col(4),
                  pl.BlockSpec((gamma.shape[0], LANES), lambda i, h: (0, h)),
                  pl.BlockSpec((1, LANES), lambda i, h: (0, 0))],
        args=[pa, pa, pa, pa, pa, gamma, norm_g.reshape(1, LANES)],
        out_spec=pl.BlockSpec((1, s, LANES), lambda i, h: (i, 0, h)),
        out_shape=jax.ShapeDtypeStruct((b, s, heads * LANES), BF16),
        scratch=[pltpu.VMEM((s, LANES), F32), pltpu.VMEM((s, LANES), F32),
                 pltpu.VMEM((2, LANES, LANES), F32)])


def _softplus(z):
    return jnp.maximum(z, 0.0) + jnp.log(1.0 + jnp.exp(-jnp.abs(z)))


def _tri_inverses(ns, r_i, c_i):
    size = ns[0].shape[0]
    blk = SUBLANES
    diag_blocks = r_i // blk == c_i // blk
    eye = jnp.where(r_i == c_i, 1.0, 0.0)
    n0 = [jnp.where(diag_blocks, n, 0.0).astype(BF16) for n in ns]
    t = [eye + n for n in n0]
    x = [jnp.dot(n, n, preferred_element_type=F32).astype(BF16) for n in n0]
    yield
    t = [tj + _mmb(tj, xj) for tj, xj in zip(t, x)]
    x = [jnp.dot(xj, xj, preferred_element_type=F32).astype(BF16) for xj in x]
    yield
    t = [tj + _mmb(tj, xj) for tj, xj in zip(t, x)]
    yield
    while blk < size:
        pair = (r_i // (2 * blk) == c_i // (2 * blk)) & (r_i // blk != c_i // blk)
        tb = [tj.astype(BF16) for tj in t]
        p = [jnp.dot(tj, jnp.where(pair, n, 0.0).astype(BF16), preferred_element_type=F32).astype(BF16)
             for tj, n in zip(tb, ns)]
        yield
        t = [tj + jnp.dot(pj, tbj, preferred_element_type=F32) for tj, pj, tbj in zip(t, p, tb)]
        yield
        blk *= 2
    return t


def _rwkv_parts(r_ref, k_ref, v_ref, lo_ref, mur_ref, muk_ref, muv_ref, mulo_ref,
                w0_ref, w2_ref, a0_ref, a2_ref, g2_ref, kk_ref, ka_ref, rk_ref, lw_ref, lb_ref,
                o_ref,
                gds_ref, qp_ref, y0_ref, m_ref, gm_ref, bon_ref,
                yf_ref, yb_ref, h_ref, lhs_ref, rhs_ref, end_ref, adec_ref, rdec_ref, gam_ref, vb_ref, *, seq):
    c = RWKV_C
    nb = RWKV_NB
    n_chunks = seq // c
    n_groups = n_chunks // nb

    def shifted(src, mu_ref, t0):
        cur = src[0, pl.ds(t0, c), :]
        before = src[0, pl.ds(pl.multiple_of(jnp.maximum(t0 - SUBLANES, 0), SUBLANES), SUBLANES), :]
        after = src[0, pl.ds(pl.multiple_of(jnp.minimum(t0 + c, seq - SUBLANES), SUBLANES), SUBLANES), :]
        row_before = before[SUBLANES - 1:SUBLANES, :] * (t0 > 0).astype(F32)
        row_after = after[0:1, :] * (t0 + c < seq).astype(F32)
        rows = lax.broadcasted_iota(jnp.int32, cur.shape, 0)
        prev = jnp.where(rows == 0, row_before, pltpu.roll(cur, 1, 0))
        nxt = jnp.where(rows == c - 1, row_after, pltpu.roll(cur, c - 1, 0))
        return cur + mu_ref[0:1, :] * (prev - cur) + mu_ref[1:2, :] * (nxt - cur)

    r_i = lax.broadcasted_iota(jnp.int32, (c, c), 0)
    c_i = lax.broadcasted_iota(jnp.int32, (c, c), 1)
    eye = r_i == c_i
    same_head = r_i // RWKV_DH == c_i // RWKV_DH
    head_sum = jnp.where(same_head, 1.0, 0.0).astype(BF16)
    incl = [c_i <= r_i, c_i >= r_i]
    incl_bf = [jnp.where(m, 1.0, 0.0).astype(BF16) for m in incl]
    r_q = lax.broadcasted_iota(jnp.int32, (2 * c, 2 * c), 0)
    c_q = lax.broadcasted_iota(jnp.int32, (2 * c, 2 * c), 1)
    r_l, c_l = r_q % c, c_q % c
    diag_r = (r_q >= c) & (r_l == c_l)
    quad = [(c_l < r_l) | diag_r, (c_l > r_l) | diag_r]
    tot_rows = [c - 1, 0]
    lane = lax.broadcasted_iota(jnp.int32, (1, LANES), 1)
    head_lanes = [lane < RWKV_DH, lane >= RWKV_DH]
    lane2 = lax.broadcasted_iota(jnp.int32, (1, 2 * LANES), 1)
    first_head2 = (lane2 % LANES) < RWKV_DH
    zeros = jnp.zeros((c, LANES), F32)

    def prep(grp):
        for u in range(nb):
            ci = grp * nb + u
            t0 = pl.multiple_of(ci * c, c)
            r = shifted(r_ref, mur_ref, t0)
            k = shifted(k_ref, muk_ref, t0)
            v = shifted(v_ref, muv_ref, t0)
            lo = shifted(lo_ref, mulo_ref, t0)
            gds_ref[pl.ds(t0, c), :] = lo[:, 2 * LANES:3 * LANES]
            kk = k * kk_ref[...]
            kk = kk * lax.rsqrt(_mm_mask(kk * kk, head_sum) + 1e-12)
            tanh_w = jnp.tanh(lo[:, 0:LANES])
            bonus = zeros
            for d in range(2):
                w = w0_ref[d:d + 1, :] + _mmb(tanh_w, w2_ref[d])
                wlog = -jnp.exp(-_softplus(-w) - 0.5)
                a = _sigmoid(a0_ref[d:d + 1, :] + _mmb(lo[:, LANES:2 * LANES], a2_ref[d]))
                kd = k * (1.0 + (a - 1.0) * ka_ref[...])
                bonus = bonus + _mm_mask(r * kd * rk_ref[...], head_sum) * v
                cum = _mask_mm2(incl_bf[d], wlog)
                mid = cum[c // 2:c // 2 + 1, :]
                tot = cum[tot_rows[d]:tot_rows[d] + 1, :]
                e1 = jnp.exp(cum - mid)
                e2 = jnp.exp(mid - cum)
                em = jnp.exp(mid)
                et = jnp.exp(tot - mid)
                a_rel = -kk * e1 * jnp.exp(-wlog)
                r_rel = r * e1
                b_inv = kk * a * e2
                k_inv = kd * e2
                j = 2 * u + d
                for h, hl in enumerate(head_lanes):
                    lhs_ref[j, h] = jnp.concatenate([jnp.where(hl, a_rel, 0.0), jnp.where(hl, r_rel, 0.0)],
                                                    axis=0).astype(BF16)
                rhs_ref[j] = jnp.concatenate([b_inv, k_inv], axis=0).astype(BF16)
                end_ref[j] = jnp.concatenate([b_inv * et, k_inv * et], axis=0).astype(BF16)
                adec_ref[j] = (a_rel * em).astype(BF16)
                rdec_ref[j] = r_rel * em
                gam_ref[j] = jnp.exp(tot)
            vb_ref[u] = v.astype(BF16)
            bon_ref[pl.ds(t0, c), :] = bonus
            yield

    def chain(grp):
        probs =[dict(ci=grp * nb + u, t0=pl.multiple_of((grp * nb + u) * c, c), d=d, u=u, j=2 * u + d)
                 for u in range(nb) for d in range(2)]
        chains = [(p, h) for p in probs for h in range(2)]
        blocks = [lax.dot_general(lhs_ref[p["j"], h], rhs_ref[p["j"]], NT, preferred_element_type=F32)
                  for p, h in chains]
        yield
        blocks = [jnp.where(quad[p["d"]], bj, 0.0) for bj, (p, h) in zip(blocks, chains)]
        a_ab = [bj[:c, :c] for bj in blocks]
        a_ak = [bj[:c, c:].astype(BF16) for bj in blocks]
        a_r = [bj[c:, :].astype(BF16) for bj in blocks]
        av = [jnp.dot(aj, vb_ref[p["u"]], preferred_element_type=F32) for aj, (p, h) in zip(a_ak, chains)]
        yield
        t = yield from _tri_inverses(a_ab, r_i, c_i)
        wu = [jnp.dot(tj.astype(BF16), jnp.concatenate([adec_ref[p["j"]], avj.astype(BF16)], axis=1),
                      preferred_element_type=F32) for tj, avj, (p, h) in zip(t, av, chains)]
        yield
        lower = [jnp.concatenate([zeros.astype(BF16), vb_ref[p["u"]]], axis=1) for p in probs]
        qy = [jnp.dot(arj, jnp.concatenate([wuj.astype(BF16), lower[j // 2]], axis=0),
                      preferred_element_type=F32) for j, (arj, wuj) in enumerate(zip(a_r, wu))]
        yield
        for j, p in enumerate(probs):
            d, ci, t0 = p["d"], p["ci"], p["t0"]
            wu_p = jnp.where(first_head2, wu[2 * j], wu[2 * j + 1])
            qy_p = jnp.where(first_head2, qy[2 * j], qy[2 * j + 1])
            qp_ref[d, pl.ds(t0, c), :] = (rdec_ref[j] + qy_p[:, :c]).astype(BF16)
            y0_ref[d, pl.ds(t0, c), :] = qy_p[:, c:]
            mg = lax.dot_general(end_ref[j], jnp.concatenate([wu_p.astype(BF16), lower[j]], axis=0),
                                 TN, preferred_element_type=F32)
            m = jnp.where(same_head, mg[:, :c], 0.0) + jnp.where(eye, gam_ref[j], 0.0)
            m_hi, m_lo = _split2(m)
            m_ref[d, ci] = jnp.concatenate([m_hi, m_hi, m_lo], axis=1)
            gm_ref[d, ci] = jnp.where(same_head, mg[:, c:], 0.0)
            if j % 2:
                yield

    def phase_a(grp):
        yield from prep(grp)
        yield from chain(grp)

    h_ref[...] = jnp.zeros_like(h_ref)
    inv_dh = 1.0 / RWKV_DH

    def finish(ys, t0s):
        yc = [y - _mm_mask(y, head_sum) * inv_dh for y in ys]
        var = [_mm_mask(x * x, head_sum) * inv_dh for x in yc]
        gate = [_mmb(_sigmoid(gds_ref[pl.ds(t0, c), :]), g2_ref[...]) for t0 in t0s]
        for x, vr, gt, t0 in zip(yc, var, gate, t0s):
            yn = x * lax.rsqrt(vr + RWKV_GN_EPS) * lw_ref[...] + lb_ref[...]
            o_ref[0, pl.ds(t0, c), :] = ((yn + bon_ref[pl.ds(t0, c), :]) * gt).astype(o_ref.dtype)

    def state_step(i, second_half):
        cis = [i, n_chunks - 1 - i]
        t0s = [pl.multiple_of(ci * c, c) for ci in cis]
        hs = [_split2(h_ref[d]) for d in range(2)]
        y = [jnp.dot(qp_ref[d, pl.ds(t0s[d], c), :], hs[d][0], preferred_element_type=F32)
             + y0_ref[d, pl.ds(t0s[d], c), :] for d in range(2)]
        hn = [jnp.dot(m_ref[d, cis[d]], jnp.concatenate([hs[d][0], hs[d][1], hs[d][0]], axis=0),
                      preferred_element_type=F32) + gm_ref[d, cis[d]] for d in range(2)]
        for d in range(2):
            h_ref[d] = hn[d]
        if second_half:
            finish([y[0] + yb_ref[pl.ds(t0s[0], c), :], yf_ref[pl.ds(t0s[1], c), :] + y[1]], t0s)
        else:
            yf_ref[pl.ds(t0s[0], c), :] = y[0]
            yb_ref[pl.ds(t0s[1], c), :] = y[1]

    return phase_a, n_groups, state_step, n_chunks


def _rwkv_call(pb, mu, w0, w2, a0, a2, g2, k_k, k_a, r_k, lnx_w, lnx_b):
    b, s, n_in = pb.shape
    width = w0.shape[-1]
    pairs = width // LANES
    lo_w = n_in - 3 * width
    assert lo_w == 3 * LANES and (3 * width) % lo_w == 0 and s % (2 * RWKV_C) == 0 and s % (RWKV_C * RWKV_NB) == 0
    lo_blk = 3 * width // lo_w
    lora = w2.shape[1]
    n_chunks = s // RWKV_C
    zero = jnp.zeros((lora, width), F32)
    w2p = jnp.stack([jnp.concatenate([w2[0], zero]), jnp.concatenate([zero, w2[1]])])
    a2p = jnp.stack([jnp.concatenate([a2[0], zero]), jnp.concatenate([zero, a2[1]])])
    colv = lambda k: pl.BlockSpec((1, s, LANES), lambda i, j, k=k: (i, 0, k * pairs + j))
    mucol = lambda k: pl.BlockSpec((2, LANES), lambda i, j, k=k: (0, k * pairs + j))
    vec = pl.BlockSpec((1, LANES), lambda i, j: (0, j))
    vec2 = pl.BlockSpec((2, LANES), lambda i, j: (0, j))
    fac = pl.BlockSpec((2, LANES, LANES), lambda i, j: (0, 0, j))
    seqbuf = pltpu.VMEM((s, LANES), F32)
    n_prob = 2 * RWKV_NB
    c = RWKV_C
    staged = [pltpu.VMEM((n_prob, 2, 2 * c, LANES), BF16),
              pltpu.VMEM((n_prob, 2 * c, LANES), BF16),
              pltpu.VMEM((n_prob, 2 * c, LANES), BF16),
              pltpu.VMEM((n_prob, c, LANES), BF16),
              pltpu.VMEM((n_prob, c, LANES), F32),
              pltpu.VMEM((n_prob, 1, LANES), F32),
              pltpu.VMEM((RWKV_NB, c, LANES), BF16)]
    return dict(
        heads=pairs,
        in_specs=[colv(0), colv(1), colv(2),
                  pl.BlockSpec((1, s, lo_w), lambda i, j: (i, 0, lo_blk)),
                  mucol(0), mucol(1), mucol(2),
                  pl.BlockSpec((2, lo_w), lambda i, j: (0, lo_blk)),
                  vec2, fac, vec2, fac,
                  pl.BlockSpec((LANES, LANES), lambda i, j: (0, j)),
                  vec, vec, vec, vec, vec],
        args=[pb, pb, pb, pb, mu, mu, mu, mu, w0, w2p.astype(BF16), a0, a2p.astype(BF16), g2.astype(BF16),
              k_k.reshape(1, width), k_a.reshape(1, width), r_k.reshape(1, width),
              lnx_w.reshape(1, width), lnx_b.reshape(1, width)],
        out_spec=pl.BlockSpec((1, s, LANES), lambda i, j: (i, 0, j)),
        out_shape=jax.ShapeDtypeStruct((b, s, width), BF16),
        scratch=[seqbuf,
                 pltpu.VMEM((2, s, LANES), BF16), pltpu.VMEM((2, s, LANES), F32),
                 pltpu.VMEM((2, n_chunks, LANES, 3 * LANES), BF16),
                 pltpu.VMEM((2, n_chunks, LANES, LANES), F32),
                 seqbuf, seqbuf, seqbuf,
                 pltpu.VMEM((2, LANES, LANES), F32)] + staged)


def _ab_body(*refs, n_in, n_scratch, seq, layer):
    (h_in, r_in), (h_scr, r_scr) = n_in, n_scratch
    ins, (ya_ref, yb_ref), scr = refs[:h_in + r_in], refs[h_in + r_in:h_in + r_in + 2], refs[h_in + r_in + 2:]
    assert len(scr) == h_scr + r_scr
    h_step, h_steps, h_epilogue, h_epilogues = _hgrn_parts(*ins[:h_in], ya_ref, *scr[:h_scr], seq=seq, layer=layer)
    r_group, r_groups, r_state, r_states = _rwkv_parts(*ins[h_in:], yb_ref, *scr[h_scr:], seq=seq)
    assert h_steps == r_groups and 2 * h_epilogues == r_states

    def main(i, carry):
        side = h_step(i)
        for n, _ in enumerate(r_group(i)):
            if n % AB_INTERLEAVE == AB_INTERLEAVE - 1:
                next(side, None)
        for _ in side:
            pass
        return carry

    def first_half(i, carry):
        r_state(i, False)
        h_epilogue(i)
        return carry

    def second_half(i, carry):
        r_state(i, True)
        return carry

    lax.fori_loop(0, r_groups, main, 0)
    lax.fori_loop(0, r_states // 2, first_half, 0)
    lax.fori_loop(r_states // 2, r_states, second_half, 0, unroll=2)


def _ab_mixers(pa, pb, layer, gamma, norm_g, *rwkv_params):
    hg, rw = _hgrn_call(pa, gamma, norm_g), _rwkv_call(pb, *rwkv_params)
    assert hg["heads"] == rw["heads"]
    b, s, _ = pa.shape
    return pl.pallas_call(
        functools.partial(_ab_body, n_in=(len(hg["in_specs"]), len(rw["in_specs"])),
                          n_scratch=(len(hg["scratch"]), len(rw["scratch"])), seq=s, layer=layer),
        grid=(b, hg["heads"]),
        in_specs=hg["in_specs"] + rw["in_specs"],
        out_specs=[hg["out_spec"], rw["out_spec"]],
        out_shape=[hg["out_shape"], rw["out_shape"]],
        scratch_shapes=hg["scratch"] + rw["scratch"],
        compiler_params=_cparams(("parallel", "parallel")),
        name="hgrn2_rwkv7",
    )(*hg["args"], *rw["args"])


def _fnet_body(u_ref, cs_ref, f_ref, o_ref, z_ref, *, seq, groups, tr):
    for g in range(groups):
        z = _mmb(u_ref[0, :, g * LANES:(g + 1) * LANES], cs_ref[...])
        z_ref[0:seq, g * LANES:(g + 1) * LANES] = z[:, :LANES].astype(BF16)
        z_ref[seq:2 * seq, g * LANES:(g + 1) * LANES] = z[:, LANES:].astype(BF16)
    scale = float(1.0 / np.sqrt(float(seq * LANES)))

    def rows(i, carry):
        t0 = pl.multiple_of(i * tr, tr)
        o_ref[0, pl.ds(t0, tr), :] = (jnp.dot(f_ref[pl.ds(t0, tr), :], z_ref[...],
                                              preferred_element_type=F32) * scale).astype(o_ref.dtype)
        return carry

    lax.fori_loop(0, seq // tr, rows, 0)


def _dft_tables(seq):
    def table(n, cols):
        ang = ((jnp.arange(n, dtype=jnp.int32)[:, None] * cols[None, :]) % n).astype(F32) * float(2.0 * np.pi / n)
        return jnp.cos(ang), jnp.sin(ang)

    step = LANES if seq % LANES == 0 else 1
    c_hi, s_hi = table(seq, jnp.arange(seq // step, dtype=jnp.int32) * step)
    c_lo, s_lo = table(seq, jnp.arange(step, dtype=jnp.int32))
    cs = (c_hi[:, :, None] * c_lo[:, None, :] - s_hi[:, :, None] * s_lo[:, None, :]).reshape(seq, seq)
    ss = (s_hi[:, :, None] * c_lo[:, None, :] + c_hi[:, :, None] * s_lo[:, None, :]).reshape(seq, seq)
    cc, sc = table(LANES, jnp.arange(LANES, dtype=jnp.int32))
    return (jnp.concatenate([cs, -ss], axis=1).astype(BF16),
            jnp.concatenate([cc, sc], axis=1).astype(BF16))


def _fnet(pc):
    b, s, w = pc.shape
    groups = w // LANES
    f_tab, c_tab = _dft_tables(s)
    tr = min(s, 512)
    return pl.pallas_call(
        functools.partial(_fnet_body, seq=s, groups=groups, tr=tr),
        grid=(b,),
        in_specs=[pl.BlockSpec((1, s, w), lambda i: (i, 0, 0)),
                  _const_spec((LANES, 2 * LANES)),
                  _const_spec((s, 2 * s))],
        out_specs=pl.BlockSpec((1, s, w), lambda i: (i, 0, 0)),
        out_shape=jax.ShapeDtypeStruct((b, s, w), BF16),
        scratch_shapes=[pltpu.VMEM((2 * s, w), BF16)],
        compiler_params=_cparams(("parallel",)),
        name="fnet",
    )(pc, c_tab, f_tab)


def _conformer_body(u_ref, w_ref, b_ref, g_ref, beta_ref, o_ref, hp_ref, *, seq, width, tr):
    zeros = jnp.zeros((CONV_PAD, width), F32)
    hp_ref[0:CONV_PAD, :] = zeros
    hp_ref[CONV_PAD + seq:CONV_PAD + seq + CONV_PAD, :] = zeros

    def glu(i, carry):
        t0 = pl.multiple_of(i * tr, tr)
        val = u_ref[0, pl.ds(t0, tr), 0:width]
        gate = u_ref[0, pl.ds(t0, tr), width:2 * width]
        hp_ref[pl.ds(CONV_PAD + t0, tr), :] = val * _sigmoid(gate)
        return carry

    lax.fori_loop(0, seq // tr, glu, 0)

    def conv(i, carry):
        t0 = pl.multiple_of(i * tr, tr)
        cols = []
        for l0 in range(0, width, LANES):
            win = hp_ref[pl.ds(t0, tr + 2 * CONV_PAD), l0:l0 + LANES]
            acc = jnp.zeros((tr, LANES), F32) + b_ref[:, l0:l0 + LANES]
            for sub in range(SUBLANES):
                rot = win if sub == 0 else pltpu.roll(win, tr + 2 * CONV_PAD - sub, 0)
                for j in range(CONV_K):
                    off = j + CONV_PAD - CONV_K // 2
                    if off % SUBLANES == sub:
                        base = off - sub
                        acc = acc + w_ref[j:j + 1, l0:l0 + LANES] * rot[base:base + tr]
            cols.append(acc)
        acc = jnp.concatenate(cols, axis=1)
        mu = jnp.mean(acc, axis=-1, keepdims=True)
        xc = acc - mu
        y = xc * lax.rsqrt(jnp.mean(xc * xc, axis=-1, keepdims=True) + LN_EPS)
        o_ref[0, pl.ds(t0, tr), :] = _silu(y * g_ref[...] + beta_ref[...]).astype(o_ref.dtype)
        return carry

    lax.fori_loop(0, seq // tr, conv, 0)


def _conformer(pd, conv_w, conv_b, ln_g, ln_b):
    b, s, w2 = pd.shape
    w = w2 // 2
    tr = 64
    vec = pl.BlockSpec((1, w), lambda i: (0, 0))
    return pl.pallas_call(
        functools.partial(_conformer_body, seq=s, width=w, tr=tr),
        grid=(b,),
        in_specs=[pl.BlockSpec((1, s, w2), lambda i: (i, 0, 0)),
                  pl.BlockSpec((CONV_K, w), lambda i: (0, 0)),
                  vec, vec, vec],
        out_specs=pl.BlockSpec((1, s, w), lambda i: (i, 0, 0)),
        out_shape=jax.ShapeDtypeStruct((b, s, w), BF16),
        scratch_shapes=[pltpu.VMEM((s + 2 * CONV_PAD, w), F32)],
        compiler_params=_cparams(("parallel",)),
        name="conformer_conv",
    )(pd, conv_w, conv_b.reshape(1, w), ln_g.reshape(1, w), ln_b.reshape(1, w))


def _mix_ffn_body(x_ref, xp_ref, xn_ref, ya_ref, yap_ref, yan_ref, yb_ref, ybp_ref, ybn_ref,
                  mgate_ref, wo_ref, g_ref, sc_ref, sh_ref, gate_ref, wup_ref, cw_ref, cb_ref,
                  wd_ref, fg_ref, o_ref, act_ref, *, seq, tm, final_norm):
    rows = tm + 2 * HALO
    dff = wd_ref.shape[0]
    with_halo = lambda p, m, n: jnp.concatenate([p[0], m[0], n[0]], axis=0)
    mix = (jnp.dot(with_halo(yap_ref, ya_ref, yan_ref), wo_ref[0], preferred_element_type=F32)
           + jnp.dot(with_halo(ybp_ref, yb_ref, ybn_ref), wo_ref[1], preferred_element_type=F32))
    xa = with_halo(xp_ref, x_ref, xn_ref) + mgate_ref[0] * mix
    x_mid = xa[HALO:HALO + tm]
    hb = _modulated_norm(xa, g_ref[...], sc_ref[0], sh_ref[0]).astype(BF16)
    pos = pl.program_id(1) * tm - HALO + lax.broadcasted_iota(jnp.int32, (rows, 1), 0)
    inside = (pos >= 0) & (pos < seq)

    for lo in range(0, dff, FF_CHUNK):
        hi = lo + FF_CHUNK
        u = jnp.dot(hb, wup_ref[:, lo:hi], preferred_element_type=F32)
        u = jnp.where(inside, u, 0.0)
        v = jnp.dot(hb[HALO:HALO + tm], wup_ref[:, dff + lo:dff + hi], preferred_element_type=F32)
        u_prev = pltpu.roll(u, 1, 0)[HALO:HALO + tm]
        u_next = pltpu.roll(u, rows - 1, 0)[HALO:HALO + tm]
        uc = (cw_ref[0:1, lo:hi] * u_prev + cw_ref[1:2, lo:hi] * u[HALO:HALO + tm]
              + cw_ref[2:3, lo:hi] * u_next + cb_ref[:, lo:hi])
        act_ref[:, lo:hi] = (_silu(uc) * v).astype(BF16)
    y = x_mid + gate_ref[0] * jnp.dot(act_ref[...], wd_ref[...], preferred_element_type=F32)
    if final_norm:
        ms = jnp.mean(y * y, axis=-1, keepdims=True)
        y = y * lax.rsqrt(ms + RMS_EPS) * fg_ref[...]
    o_ref[0] = y


def _mix_ffn(x, ya, yb, mix_gate, w_out, g, scale, shift, gate, w_up, conv_w, conv_b, w_down,
             final_g, final_norm, tm):
    b, s, d = x.shape
    dff = w_down.shape[0]
    half = ya.shape[-1]
    assert dff % FF_CHUNK == 0 and s % tm == 0 and tm % HALO == 0
    hb = tm // HALO
    last = s // HALO - 1
    tiles = lambda n: [pl.BlockSpec((1, tm, n), lambda i, j: (i, j, 0)),
                       pl.BlockSpec((1, HALO, n), lambda i, j: (i, jnp.maximum(j * hb - 1, 0), 0)),
                       pl.BlockSpec((1, HALO, n), lambda i, j: (i, jnp.minimum((j + 1) * hb, last), 0))]
    row = pl.BlockSpec((1, 1, d), lambda i, j: (i, 0, 0))
    vec = pl.BlockSpec((1, d), lambda i, j: (0, 0))
    return pl.pallas_call(
        functools.partial(_mix_ffn_body, seq=s, tm=tm, final_norm=final_norm),
        grid=(b, s // tm),
        in_specs=[*tiles(d), *tiles(half), *tiles(half),
                  row, _const_spec((2, half, d)),
                  vec, row, row, row,
                  _const_spec((d, 2 * dff)),
                  _const_spec((conv_w.shape[0], dff)), _const_spec((1, dff)),
                  _const_spec((dff, d)),
                  vec],
        out_specs=pl.BlockSpec((1, tm, d), lambda i, j: (i, j, 0)),
        out_shape=jax.ShapeDtypeStruct((b, s, d), F32),
        scratch_shapes=[pltpu.VMEM((tm, dff), BF16)],
        compiler_params=_cparams(("parallel", "parallel")),
        name="mix_ffn",
    )(x, x, x, ya, ya, ya, yb, yb, yb, mix_gate, w_out.reshape(2, half, d).astype(BF16),
      g.reshape(1, d), scale, shift, gate, w_up.astype(BF16), conv_w, conv_b.reshape(1, dff),
      w_down.astype(BF16), final_g.reshape(1, d))


def kernel(x, c, ada_w, ada_b, norm_g, final_g, ab_w_in, ab_w_out, hgrn_gamma, hgrn_norm_g, rwkv_mu, rwkv_w0, rwkv_w2, rwkv_a0, rwkv_a2, rwkv_g2, rwkv_kk, rwkv_ka, rwkv_rk, rwkv_lnx_w, rwkv_lnx_b, cd_w_in, cd_w_out, dconv_w, dconv_b, dconv_ln_g, dconv_ln_b, ffn_w_up, ffn_conv_w, ffn_conv_b, ffn_w_down):
    bsz, seq, d = x.shape
    depth = ada_w.shape[0]
    a_in = 5 * (hgrn_gamma.shape[-1])
    c_width = dconv_w.shape[-1]
    tm = min(seq, 512)

    mod = _adaln(c, ada_w.reshape(depth * 2, d, 3 * d), ada_b.reshape(depth * 2, 3 * d))
    mod = mod.reshape(depth, 2, bsz, 3, 1, d)

    for l in range(depth):
        j = l // 2
        shift, scale, gate = mod[l, 0, :, 0], mod[l, 0, :, 1], mod[l, 0, :, 2]
        if l % 2 == 0:
            pa, pb = _inproj(x, norm_g[l, 0], scale, shift, ab_w_in[j],
                             (a_in, ab_w_in.shape[-1] - a_in), (F32, F32), tm)
            y1, y2 = _ab_mixers(pa, pb, l, hgrn_gamma, hgrn_norm_g[j],
                                rwkv_mu[j], rwkv_w0[j], rwkv_w2[j], rwkv_a0[j], rwkv_a2[j], rwkv_g2[j],
                                rwkv_kk[j], rwkv_ka[j], rwkv_rk[j], rwkv_lnx_w[j], rwkv_lnx_b[j])
            w_out = ab_w_out[j]
        else:
            pc, pd = _inproj(x, norm_g[l, 0], scale, shift, cd_w_in[j],
                             (cd_w_in.shape[-1] - 2 * c_width, 2 * c_width), (BF16, F32), tm)
            y1 = _fnet(pc)
            y2 = _conformer(pd, dconv_w[j], dconv_b[j], dconv_ln_g[j], dconv_ln_b[j])
            w_out = cd_w_out[j]
        shift, scale, ffn_gate = mod[l, 1, :, 0], mod[l, 1, :, 1], mod[l, 1, :, 2]
        x = _mix_ffn(x, y1, y2, gate, w_out, norm_g[l, 1], scale, shift, ffn_gate, ffn_w_up[l],
                     ffn_conv_w[l], ffn_conv_b[l], ffn_w_down[l], final_g, l == depth - 1, tm)
    return x
```

```python
import functools

import numpy as np
import jax
import jax.numpy as jnp
from jax import lax
from jax.experimental import pallas as pl
from jax.experimental.pallas import tpu as pltpu

F32 = jnp.float32
BF16 = jnp.bfloat16
HI = lax.Precision.HIGHEST

LANES = 128
SUBLANES = 8
VMEM_LIMIT_BYTES = 56 * 1024 * 1024

RMS_EPS = 1e-6
LN_EPS = 1e-5
RWKV_GN_EPS = 64e-5
RWKV_DH = 64
CONV_K = 31
CONV_PAD = 16
HGRN_C = 64
HGRN_NB = 8
RWKV_C = 128
RWKV_NB = 4
AB_INTERLEAVE = 4
EPILOGUE_ROWS = 256
FF_CHUNK = 256
HALO = 2 * SUBLANES

NN = (((1,), (0,)), ((), ()))
NT = (((1,), (1,)), ((), ()))
TN = (((0,), (0,)), ((), ()))


def _mmf(a, b, dims=NN):
    return lax.dot_general(a, b, dims, precision=HI, preferred_element_type=F32)


def _mmb(a, b, dims=NN):
    return lax.dot_general(a.astype(BF16), b.astype(BF16), dims, preferred_element_type=F32)


def _sigmoid(x):
    return jax.nn.sigmoid(x)


def _silu(x):
    return x * jax.nn.sigmoid(x)


def _cparams(sem):
    return pltpu.CompilerParams(dimension_semantics=sem, vmem_limit_bytes=VMEM_LIMIT_BYTES)


def _const_spec(shape):
    nd = len(shape)
    return pl.BlockSpec(shape, lambda *_: (0,) * nd, pipeline_mode=pl.Buffered(1))


def _adaln_body(c_ref, w_ref, b_ref, o_ref):
    o_ref[0] = _mmf(_silu(c_ref[...]), w_ref[0]) + b_ref[0]


def _adaln(c, ada_w, ada_b):
    k, d, d3 = ada_w.shape
    b = c.shape[0]
    return pl.pallas_call(
        _adaln_body,
        grid=(k, d3 // d),
        in_specs=[pl.BlockSpec((b, d), lambda i, j: (0, 0)),
                  pl.BlockSpec((1, d, d), lambda i, j: (i, 0, j)),
                  pl.BlockSpec((1, 1, d), lambda i, j: (i, 0, j))],
        out_specs=pl.BlockSpec((1, b, d), lambda i, j: (i, 0, j)),
        out_shape=jax.ShapeDtypeStruct((k, b, d3), F32),
        compiler_params=_cparams(("parallel", "parallel")),
        name="adaln",
    )(c, ada_w, ada_b.reshape(k, 1, d3))


def _modulated_norm(x, g, scale, shift):
    ms = jnp.mean(x * x, axis=-1, keepdims=True)
    return (x * lax.rsqrt(ms + RMS_EPS) * g) * (1.0 + scale) + shift


def _inproj_body(x_ref, g_ref, sc_ref, sh_ref, w_ref, *o_refs):
    hb = _modulated_norm(x_ref[0], g_ref[...], sc_ref[0], sh_ref[0]).astype(BF16)
    off = 0
    for o_ref in o_refs:
        n = o_ref.shape[-1]
        o_ref[0] = jnp.dot(hb, w_ref[:, off:off + n], preferred_element_type=F32).astype(o_ref.dtype)
        off += n


def _inproj(x, g, scale, shift, w, splits, dtypes, tm):
    b, s, d = x.shape
    n = w.shape[1]
    assert sum(splits) == n and s % tm == 0
    row = pl.BlockSpec((1, 1, d), lambda i, j: (i, 0, 0))
    return pl.pallas_call(
        _inproj_body,
        grid=(b, s // tm),
        in_specs=[pl.BlockSpec((1, tm, d), lambda i, j: (i, j, 0)),
                  pl.BlockSpec((1, d), lambda i, j: (0, 0)),
                  row, row,
                  _const_spec((d, n))],
        out_specs=[pl.BlockSpec((1, tm, m), lambda i, j: (i, j, 0)) for m in splits],
        out_shape=[jax.ShapeDtypeStruct((b, s, m), dt) for m, dt in zip(splits, dtypes)],
        compiler_params=_cparams(("parallel", "parallel")),
        name="inproj",
    )(x, g.reshape(1, d), scale, shift, w.astype(BF16))


def _split2(x):
    hi = x.astype(BF16)
    return hi, (x - hi.astype(F32)).astype(BF16)


def _mask_mm2(mask_bf16, x):
    n = x.shape[1]
    out = jnp.dot(mask_bf16, jnp.concatenate(_split2(x), axis=1), preferred_element_type=F32)
    return out[:, :n] + out[:, n:]


def _mm_mask(x, mask_bf16):
    return jnp.dot(x.astype(BF16), mask_bf16, preferred_element_type=F32)


def _hgrn_parts(q_ref, ff_ref, fb_ref, i_ref, g_ref, gam_ref, ng_ref, o_ref,
                of_ref, ob_ref, st_ref, *, seq, layer):
    c = HGRN_C
    nb = HGRN_NB
    n_chunks = seq // c
    rows = [gam_ref[pl.ds(i, 1), :] for i in range(gam_ref.shape[0])]
    mx = functools.reduce(jnp.maximum, rows)
    es = [jnp.exp(r - mx) for r in rows]
    lb = sum(es[:layer + 1]) / sum(es)

    r_i = lax.broadcasted_iota(jnp.int32, (c, c), 0)
    c_i = lax.broadcasted_iota(jnp.int32, (c, c), 1)
    masks = [c_i <= r_i, c_i >= r_i]
    masks_bf = [jnp.where(m, 1.0, 0.0).astype(BF16) for m in masks]
    tot_rows = [c - 1, 0]
    fraw_refs = [ff_ref, fb_ref]
    out_refs = [of_ref, ob_ref]
    st_ref[...] = jnp.zeros_like(st_ref)

    def body(i):
        probs = []
        for d in range(2):
            for u in range(nb):
                ci = i * nb + u
                probs.append((d, pl.multiple_of((ci if d == 0 else n_chunks - 1 - ci) * c, c)))
        q = [q_ref[0, pl.ds(t0, c), :] for d, t0 in probs]
        v = [i_ref[0, pl.ds(t0, c), :] for d, t0 in probs]
        f = [lb + (1.0 - lb) * _sigmoid(fraw_refs[d][0, pl.ds(t0, c), :]) for d, t0 in probs]
        cum = [_mask_mm2(masks_bf[d], jnp.log(fj)) for (d, t0), fj in zip(probs, f)]
        yield
        k = [1.0 - fj for fj in f]
        mid = [cj[c // 2:c // 2 + 1, :] for cj in cum]
        tot = [cj[tot_rows[d]:tot_rows[d] + 1, :] for (d, t0), cj in zip(probs, cum)]
        scores = [_mmb(qj * jnp.exp(cj - mj), kj * jnp.exp(mj - cj), NT)
                  for qj, kj, cj, mj in zip(q, k, cum, mid)]
        yield
        scores = [jnp.where(masks[d], sj, 0.0) for (d, t0), sj in zip(probs, scores)]
        intra = [_mmb(sj, vj) for sj, vj in zip(scores, v)]
        yield
        kv = [_mmb(vj, kj * jnp.exp(tj - cj), TN) for vj, kj, tj, cj in zip(v, k, tot, cum)]
        q_dec = [qj * jnp.exp(cj) for qj, cj in zip(q, cum)]
        dec = [jnp.exp(tj) for tj in tot]
        yield
        for d in range(2):
            st = st_ref[d]
            for j, (dj, t0) in enumerate(probs):
                if dj == d:
                    out_refs[d][pl.ds(t0, c), :] = intra[j] + _mmb(q_dec[j], st, NT)
                    st = st * dec[j] + kv[j]
            st_ref[d] = st
            yield

    te = min(seq, EPILOGUE_ROWS)

    def epilogue(i):
        t0 = pl.multiple_of(i * te, te)
        o = of_ref[pl.ds(t0, te), :] + ob_ref[pl.ds(t0, te), :]
        ms = jnp.mean(o * o, axis=-1, keepdims=True)
        y = o * lax.rsqrt(ms + RMS_EPS) * ng_ref[...]
        o_ref[0, pl.ds(t0, te), :] = (y * _silu(g_ref[0, pl.ds(t0, te), :])).astype(o_ref.dtype)

    return body, n_chunks // nb, epilogue, seq // te


def _hgrn_call(pa, gamma, norm_g):
    b, s, w5 = pa.shape
    heads = w5 // 5 // LANES
    assert s % (HGRN_C * HGRN_NB) == 0
    col = lambda k: pl.BlockSpec((1, s, LANES), lambda i, h, k=k: (i, 0, k * heads + h))
    return dict(
        heads=heads,
        in_specs=[col(0), col(1), col(2), col(3), col(4),
                  pl.BlockSpec((gamma.shape[0], LANES), lambda i, h: (0, h)),
                  pl.BlockSpec((1, LANES), lambda i, h: (0, 0))],
        args=[pa, pa, pa, pa, pa, gamma, norm_g.reshape(1, LANES)],
        out_spec=pl.BlockSpec((1, s, LANES), lambda i, h: (i, 0, h)),
        out_shape=jax.ShapeDtypeStruct((b, s, heads * LANES), BF16),
        scratch=[pltpu.VMEM((s, LANES), F32), pltpu.VMEM((s, LANES), F32),
                 pltpu.VMEM((2, LANES, LANES), F32)])


def _softplus(z):
    return jnp.maximum(z, 0.0) + jnp.log(1.0 + jnp.exp(-jnp.abs(z)))


def _tri_inverses(ns, r_i, c_i):
    size = ns[0].shape[0]
    blk = SUBLANES
    diag_blocks = r_i // blk == c_i // blk
    eye = jnp.where(r_i == c_i, 1.0, 0.0)
    n0 = [jnp.where(diag_blocks, n, 0.0).astype(BF16) for n in ns]
    t = [eye + n for n in n0]
    x = [jnp.dot(n, n, preferred_element_type=F32).astype(BF16) for n in n0]
    yield
    t = [tj + _mmb(tj, xj) for tj, xj in zip(t, x)]
    x = [jnp.dot(xj, xj, preferred_element_type=F32).astype(BF16) for xj in x]
    yield
    t = [tj + _mmb(tj, xj) for tj, xj in zip(t, x)]
    yield
    while blk < size:
        pair = (r_i // (2 * blk) == c_i // (2 * blk)) & (r_i // blk != c_i // blk)
        tb = [tj.astype(BF16) for tj in t]
        p = [jnp.dot(tj, jnp.where(pair, n, 0.0).astype(BF16), preferred_element_type=F32).astype(BF16)
             for tj, n in zip(tb, ns)]
        yield
        t = [tj + jnp.dot(pj, tbj, preferred_element_type=F32) for tj, pj, tbj in zip(t, p, tb)]
        yield
        blk *= 2
    return t


def _rwkv_parts(r_ref, k_ref, v_ref, lo_ref, mur_ref, muk_ref, muv_ref, mulo_ref,
                w0_ref, w2_ref, a0_ref, a2_ref, g2_ref, kk_ref, ka_ref, rk_ref, lw_ref, lb_ref,
                o_ref,
                gds_ref, qp_ref, y0_ref, m_ref, gm_ref, bon_ref,
                yf_ref, yb_ref, h_ref, lhs_ref, rhs_ref, end_ref, adec_ref, rdec_ref, gam_ref, vb_ref, *, seq):
    c = RWKV_C
    nb = RWKV_NB
    n_chunks = seq // c
    n_groups = n_chunks // nb

    def shifted(src, mu_ref, t0):
        cur = src[0, pl.ds(t0, c), :]
        before = src[0, pl.ds(pl.multiple_of(jnp.maximum(t0 - SUBLANES, 0), SUBLANES), SUBLANES), :]
        after = src[0, pl.ds(pl.multiple_of(jnp.minimum(t0 + c, seq - SUBLANES), SUBLANES), SUBLANES), :]
        row_before = before[SUBLANES - 1:SUBLANES, :] * (t0 > 0).astype(F32)
        row_after = after[0:1, :] * (t0 + c < seq).astype(F32)
        rows = lax.broadcasted_iota(jnp.int32, cur.shape, 0)
        prev = jnp.where(rows == 0, row_before, pltpu.roll(cur, 1, 0))
        nxt = jnp.where(rows == c - 1, row_after, pltpu.roll(cur, c - 1, 0))
        return cur + mu_ref[0:1, :] * (prev - cur) + mu_ref[1:2, :] * (nxt - cur)

    r_i = lax.broadcasted_iota(jnp.int32, (c, c), 0)
    c_i = lax.broadcasted_iota(jnp.int32, (c, c), 1)
    eye = r_i == c_i
    same_head = r_i // RWKV_DH == c_i // RWKV_DH
    head_sum = jnp.where(same_head, 1.0, 0.0).astype(BF16)
    incl = [c_i <= r_i, c_i >= r_i]
    incl_bf = [jnp.where(m, 1.0, 0.0).astype(BF16) for m in incl]
    r_q = lax.broadcasted_iota(jnp.int32, (2 * c, 2 * c), 0)
    c_q = lax.broadcasted_iota(jnp.int32, (2 * c, 2 * c), 1)
    r_l, c_l = r_q % c, c_q % c
    diag_r = (r_q >= c) & (r_l == c_l)
    quad = [(c_l < r_l) | diag_r, (c_l > r_l) | diag_r]
    tot_rows = [c - 1, 0]
    lane = lax.broadcasted_iota(jnp.int32, (1, LANES), 1)
    head_lanes = [lane < RWKV_DH, lane >= RWKV_DH]
    lane2 = lax.broadcasted_iota(jnp.int32, (1, 2 * LANES), 1)
    first_head2 = (lane2 % LANES) < RWKV_DH
    zeros = jnp.zeros((c, LANES), F32)

    def prep(grp):
        for u in range(nb):
            ci = grp * nb + u
            t0 = pl.multiple_of(ci * c, c)
            r = shifted(r_ref, mur_ref, t0)
            k = shifted(k_ref, muk_ref, t0)
            v = shifted(v_ref, muv_ref, t0)
            lo = shifted(lo_ref, mulo_ref, t0)
            gds_ref[pl.ds(t0, c), :] = lo[:, 2 * LANES:3 * LANES]
            kk = k * kk_ref[...]
            kk = kk * lax.rsqrt(_mm_mask(kk * kk, head_sum) + 1e-12)
            tanh_w = jnp.tanh(lo[:, 0:LANES])
            bonus = zeros
            for d in range(2):
                w = w0_ref[d:d + 1, :] + _mmb(tanh_w, w2_ref[d])
                wlog = -jnp.exp(-_softplus(-w) - 0.5)
                a = _sigmoid(a0_ref[d:d + 1, :] + _mmb(lo[:, LANES:2 * LANES], a2_ref[d]))
                kd = k * (1.0 + (a - 1.0) * ka_ref[...])
                bonus = bonus + r * kd * rk_ref[...]
                cum = _mask_mm2(incl_bf[d], wlog)
                mid = cum[c // 2:c // 2 + 1, :]
                tot = cum[tot_rows[d]:tot_rows[d] + 1, :]
                e1 = jnp.exp(cum - mid)
                e2 = jnp.exp(mid - cum)
                em = jnp.exp(mid)
                et = jnp.exp(tot - mid)
                a_rel = -kk * e1 * jnp.exp(-wlog)
                r_rel = r * e1
                b_inv = kk * a * e2
                k_inv = kd * e2
                j = 2 * u + d
                for h, hl in enumerate(head_lanes):
                    lhs_ref[j, h] = jnp.concatenate([jnp.where(hl, a_rel, 0.0), jnp.where(hl, r_rel, 0.0)],
                                                    axis=0).astype(BF16)
                rhs_ref[j] = jnp.concatenate([b_inv, k_inv], axis=0).astype(BF16)
                end_ref[j] = jnp.concatenate([b_inv * et, k_inv * et], axis=0).astype(BF16)
                adec_ref[j] = (a_rel * em).astype(BF16)
                rdec_ref[j] = r_rel * em
                gam_ref[j] = jnp.exp(tot)
            vb_ref[u] = v.astype(BF16)
            bon_ref[pl.ds(t0, c), :] = _mm_mask(bonus, head_sum) * v
            yield

    def chain(grp):
        probs =[dict(ci=grp * nb + u, t0=pl.multiple_of((grp * nb + u) * c, c), d=d, u=u, j=2 * u + d)
                 for u in range(nb) for d in range(2)]
        chains = [(p, h) for p in probs for h in range(2)]
        blocks = [lax.dot_general(lhs_ref[p["j"], h], rhs_ref[p["j"]], NT, preferred_element_type=F32)
                  for p, h in chains]
        yield
        blocks = [jnp.where(quad[p["d"]], bj, 0.0) for bj, (p, h) in zip(blocks, chains)]
        a_ab = [bj[:c, :c] for bj in blocks]
        a_ak = [bj[:c, c:].astype(BF16) for bj in blocks]
        a_r = [bj[c:, :].astype(BF16) for bj in blocks]
        av = [jnp.dot(aj, vb_ref[p["u"]], preferred_element_type=F32) for aj, (p, h) in zip(a_ak, chains)]
        yield
        t = yield from _tri_inverses(a_ab, r_i, c_i)
        wu = [jnp.dot(tj.astype(BF16), jnp.concatenate([adec_ref[p["j"]], avj.astype(BF16)], axis=1),
                      preferred_element_type=F32) for tj, avj, (p, h) in zip(t, av, chains)]
        yield
        lower = [jnp.concatenate([zeros.astype(BF16), vb_ref[p["u"]]], axis=1) for p in probs]
        qy = [jnp.dot(arj, jnp.concatenate([wuj.astype(BF16), lower[j // 2]], axis=0),
                      preferred_element_type=F32) for j, (arj, wuj) in enumerate(zip(a_r, wu))]
        yield
        for j, p in enumerate(probs):
            d, ci, t0 = p["d"], p["ci"], p["t0"]
            wu_p = jnp.where(first_head2, wu[2 * j], wu[2 * j + 1])
            qy_p = jnp.where(first_head2, qy[2 * j], qy[2 * j + 1])
            qp_ref[d, pl.ds(t0, c), :] = (rdec_ref[j] + qy_p[:, :c]).astype(BF16)
            y0_ref[d, pl.ds(t0, c), :] = qy_p[:, c:]
            mg = lax.dot_general(end_ref[j], jnp.concatenate([wu_p.astype(BF16), lower[j]], axis=0),
                                 TN, preferred_element_type=F32)
            m = jnp.where(same_head, mg[:, :c], 0.0) + jnp.where(eye, gam_ref[j], 0.0)
            m_hi, m_lo = _split2(m)
            m_ref[d, ci] = jnp.concatenate([m_hi, m_hi, m_lo], axis=1)
            gm_ref[d, ci] = jnp.where(same_head, mg[:, c:], 0.0)
            if j % 2:
                yield

    def phase_a(grp):
        yield from prep(grp)
        yield from chain(grp)

    h_ref[...] = jnp.zeros_like(h_ref)
    inv_dh = 1.0 / RWKV_DH

    def finish(ys, t0s):
        yc = [y - _mm_mask(y, head_sum) * inv_dh for y in ys]
        var = [_mm_mask(x * x, head_sum) * inv_dh for x in yc]
        gate = [_mmb(_sigmoid(gds_ref[pl.ds(t0, c), :]), g2_ref[...]) for t0 in t0s]
        for x, vr, gt, t0 in zip(yc, var, gate, t0s):
            yn = x * lax.rsqrt(vr + RWKV_GN_EPS) * lw_ref[...] + lb_ref[...]
            o_ref[0, pl.ds(t0, c), :] = ((yn + bon_ref[pl.ds(t0, c), :]) * gt).astype(o_ref.dtype)

    def state_step(i, second_half):
        cis = [i, n_chunks - 1 - i]
        t0s = [pl.multiple_of(ci * c, c) for ci in cis]
        hs = [_split2(h_ref[d]) for d in range(2)]
        y = [jnp.dot(qp_ref[d, pl.ds(t0s[d], c), :], hs[d][0], preferred_element_type=F32)
             + y0_ref[d, pl.ds(t0s[d], c), :] for d in range(2)]
        hn = [jnp.dot(m_ref[d, cis[d]], jnp.concatenate([hs[d][0], hs[d][1], hs[d][0]], axis=0),
                      preferred_element_type=F32) + gm_ref[d, cis[d]] for d in range(2)]
        for d in range(2):
            h_ref[d] = hn[d]
        if second_half:
            finish([y[0] + yb_ref[pl.ds(t0s[0], c), :], yf_ref[pl.ds(t0s[1], c), :] + y[1]], t0s)
        else:
            yf_ref[pl.ds(t0s[0], c), :] = y[0]
            yb_ref[pl.ds(t0s[1], c), :] = y[1]

    return phase_a, n_groups, state_step, n_chunks


def _rwkv_call(pb, mu, w0, w2, a0, a2, g2, k_k, k_a, r_k, lnx_w, lnx_b):
    b, s, n_in = pb.shape
    width = w0.shape[-1]
    pairs = width // LANES
    lo_w = n_in - 3 * width
    assert lo_w == 3 * LANES and (3 * width) % lo_w == 0 and s % (2 * RWKV_C) == 0 and s % (RWKV_C * RWKV_NB) == 0
    lo_blk = 3 * width // lo_w
    lora = w2.shape[1]
    n_chunks = s // RWKV_C
    zero = jnp.zeros((lora, width), F32)
    w2p = jnp.stack([jnp.concatenate([w2[0], zero]), jnp.concatenate([zero, w2[1]])])
    a2p = jnp.stack([jnp.concatenate([a2[0], zero]), jnp.concatenate([zero, a2[1]])])
    colv = lambda k: pl.BlockSpec((1, s, LANES), lambda i, j, k=k: (i, 0, k * pairs + j))
    mucol = lambda k: pl.BlockSpec((2, LANES), lambda i, j, k=k: (0, k * pairs + j))
    vec = pl.BlockSpec((1, LANES), lambda i, j: (0, j))
    vec2 = pl.BlockSpec((2, LANES), lambda i, j: (0, j))
    fac = pl.BlockSpec((2, LANES, LANES), lambda i, j: (0, 0, j))
    seqbuf = pltpu.VMEM((s, LANES), F32)
    n_prob = 2 * RWKV_NB
    c = RWKV_C
    staged = [pltpu.VMEM((n_prob, 2, 2 * c, LANES), BF16),
              pltpu.VMEM((n_prob, 2 * c, LANES), BF16),
              pltpu.VMEM((n_prob, 2 * c, LANES), BF16),
              pltpu.VMEM((n_prob, c, LANES), BF16),
              pltpu.VMEM((n_prob, c, LANES), F32),
              pltpu.VMEM((n_prob, 1, LANES), F32),
              pltpu.VMEM((RWKV_NB, c, LANES), BF16)]
    return dict(
        heads=pairs,
        in_specs=[colv(0), colv(1), colv(2),
                  pl.BlockSpec((1, s, lo_w), lambda i, j: (i, 0, lo_blk)),
                  mucol(0), mucol(1), mucol(2),
                  pl.BlockSpec((2, lo_w), lambda i, j: (0, lo_blk)),
                  vec2, fac, vec2, fac,
                  pl.BlockSpec((LANES, LANES), lambda i, j: (0, j)),
                  vec, vec, vec, vec, vec],
        args=[pb, pb, pb, pb, mu, mu, mu, mu, w0, w2p.astype(BF16), a0, a2p.astype(BF16), g2.astype(BF16),
              k_k.reshape(1, width), k_a.reshape(1, width), r_k.reshape(1, width),
              lnx_w.reshape(1, width), lnx_b.reshape(1, width)],
        out_spec=pl.BlockSpec((1, s, LANES), lambda i, j: (i, 0, j)),
        out_shape=jax.ShapeDtypeStruct((b, s, width), BF16),
        scratch=[seqbuf,
                 pltpu.VMEM((2, s, LANES), BF16), pltpu.VMEM((2, s, LANES), F32),
                 pltpu.VMEM((2, n_chunks, LANES, 3 * LANES), BF16),
                 pltpu.VMEM((2, n_chunks, LANES, LANES), F32),
                 seqbuf, seqbuf, seqbuf,
                 pltpu.VMEM((2, LANES, LANES), F32)] + staged)


def _ab_body(*refs, n_in, n_scratch, seq, layer):
    (h_in, r_in), (h_scr, r_scr) = n_in, n_scratch
    ins, (ya_ref, yb_ref), scr = refs[:h_in + r_in], refs[h_in + r_in:h_in + r_in + 2], refs[h_in + r_in + 2:]
    assert len(scr) == h_scr + r_scr
    h_step, h_steps, h_epilogue, h_epilogues = _hgrn_parts(*ins[:h_in], ya_ref, *scr[:h_scr], seq=seq, layer=layer)
    r_group, r_groups, r_state, r_states = _rwkv_parts(*ins[h_in:], yb_ref, *scr[h_scr:], seq=seq)
    assert h_steps == r_groups and 2 * h_epilogues == r_states

    def main(i, carry):
        side = h_step(i)
        for n, _ in enumerate(r_group(i)):
            if n % AB_INTERLEAVE == AB_INTERLEAVE - 1:
                next(side, None)
        for _ in side:
            pass
        return carry

    def first_half(i, carry):
        r_state(i, False)
        h_epilogue(i)
        return carry

    def second_half(i, carry):
        r_state(i, True)
        return carry

    lax.fori_loop(0, r_groups, main, 0)
    lax.fori_loop(0, r_states // 2, first_half, 0)
    lax.fori_loop(r_states // 2, r_states, second_half, 0, unroll=2)


def _ab_mixers(pa, pb, layer, gamma, norm_g, *rwkv_params):
    hg, rw = _hgrn_call(pa, gamma, norm_g), _rwkv_call(pb, *rwkv_params)
    assert hg["heads"] == rw["heads"]
    b, s, _ = pa.shape
    return pl.pallas_call(
        functools.partial(_ab_body, n_in=(len(hg["in_specs"]), len(rw["in_specs"])),
                          n_scratch=(len(hg["scratch"]), len(rw["scratch"])), seq=s, layer=layer),
        grid=(b, hg["heads"]),
        in_specs=hg["in_specs"] + rw["in_specs"],
        out_specs=[hg["out_spec"], rw["out_spec"]],
        out_shape=[hg["out_shape"], rw["out_shape"]],
        scratch_shapes=hg["scratch"] + rw["scratch"],
        compiler_params=_cparams(("parallel", "parallel")),
        name="hgrn2_rwkv7",
    )(*hg["args"], *rw["args"])


def _fnet_body(u_ref, cs_ref, f_ref, o_ref, z_ref, *, seq, groups, tr):
    for g in range(groups):
        z = _mmb(u_ref[0, :, g * LANES:(g + 1) * LANES], cs_ref[...])
        z_ref[0:seq, g * LANES:(g + 1) * LANES] = z[:, :LANES].astype(BF16)
        z_ref[seq:2 * seq, g * LANES:(g + 1) * LANES] = z[:, LANES:].astype(BF16)
    scale = float(1.0 / np.sqrt(float(seq * LANES)))

    def rows(i, carry):
        t0 = pl.multiple_of(i * tr, tr)
        o_ref[0, pl.ds(t0, tr), :] = (jnp.dot(f_ref[pl.ds(t0, tr), :], z_ref[...],
                                              preferred_element_type=F32) * scale).astype(o_ref.dtype)
        return carry

    lax.fori_loop(0, seq // tr, rows, 0)


def _dft_tables(seq):
    def table(n, cols):
        ang = ((jnp.arange(n, dtype=jnp.int32)[:, None] * cols[None, :]) % n).astype(F32) * float(2.0 * np.pi / n)
        return jnp.cos(ang), jnp.sin(ang)

    step = LANES if seq % LANES == 0 else 1
    c_hi, s_hi = table(seq, jnp.arange(seq // step, dtype=jnp.int32) * step)
    c_lo, s_lo = table(seq, jnp.arange(step, dtype=jnp.int32))
    cs = (c_hi[:, :, None] * c_lo[:, None, :] - s_hi[:, :, None] * s_lo[:, None, :]).reshape(seq, seq)
    ss = (s_hi[:, :, None] * c_lo[:, None, :] + c_hi[:, :, None] * s_lo[:, None, :]).reshape(seq, seq)
    cc, sc = table(LANES, jnp.arange(LANES, dtype=jnp.int32))
    return (jnp.concatenate([cs, -ss], axis=1).astype(BF16),
            jnp.concatenate([cc, sc], axis=1).astype(BF16))


def _fnet(pc):
    b, s, w = pc.shape
    groups = w // LANES
    f_tab, c_tab = _dft_tables(s)
    tr = min(s, 512)
    return pl.pallas_call(
        functools.partial(_fnet_body, seq=s, groups=groups, tr=tr),
        grid=(b,),
        in_specs=[pl.BlockSpec((1, s, w), lambda i: (i, 0, 0)),
                  _const_spec((LANES, 2 * LANES)),
                  _const_spec((s, 2 * s))],
        out_specs=pl.BlockSpec((1, s, w), lambda i: (i, 0, 0)),
        out_shape=jax.ShapeDtypeStruct((b, s, w), BF16),
        scratch_shapes=[pltpu.VMEM((2 * s, w), BF16)],
        compiler_params=_cparams(("parallel",)),
        name="fnet",
    )(pc, c_tab, f_tab)


def _conformer_body(u_ref, w_ref, b_ref, g_ref, beta_ref, o_ref, hp_ref, *, seq, width, tr):
    zeros = jnp.zeros((CONV_PAD, width), F32)
    hp_ref[0:CONV_PAD, :] = zeros
    hp_ref[CONV_PAD + seq:CONV_PAD + seq + CONV_PAD, :] = zeros

    def glu(i, carry):
        t0 = pl.multiple_of(i * tr, tr)
        val = u_ref[0, pl.ds(t0, tr), 0:width]
        gate = u_ref[0, pl.ds(t0, tr), width:2 * width]
        hp_ref[pl.ds(CONV_PAD + t0, tr), :] = val * _sigmoid(gate)
        return carry

    lax.fori_loop(0, seq // tr, glu, 0)

    def conv(i, carry):
        t0 = pl.multiple_of(i * tr, tr)
        cols = []
        for l0 in range(0, width, LANES):
            win = hp_ref[pl.ds(t0, tr + 2 * CONV_PAD), l0:l0 + LANES]
            acc = jnp.zeros((tr, LANES), F32) + b_ref[:, l0:l0 + LANES]
            for sub in range(SUBLANES):
                rot = win if sub == 0 else pltpu.roll(win, tr + 2 * CONV_PAD - sub, 0)
                for j in range(CONV_K):
                    off = j + CONV_PAD - CONV_K // 2
                    if off % SUBLANES == sub:
                        base = off - sub
                        acc = acc + w_ref[j:j + 1, l0:l0 + LANES] * rot[base:base + tr]
            cols.append(acc)
        acc = jnp.concatenate(cols, axis=1)
        mu = jnp.mean(acc, axis=-1, keepdims=True)
        xc = acc - mu
        y = xc * lax.rsqrt(jnp.mean(xc * xc, axis=-1, keepdims=True) + LN_EPS)
        o_ref[0, pl.ds(t0, tr), :] = _silu(y * g_ref[...] + beta_ref[...]).astype(o_ref.dtype)
        return carry

    lax.fori_loop(0, seq // tr, conv, 0)


def _conformer(pd, conv_w, conv_b, ln_g, ln_b):
    b, s, w2 = pd.shape
    w = w2 // 2
    tr = 64
    vec = pl.BlockSpec((1, w), lambda i: (0, 0))
    return pl.pallas_call(
        functools.partial(_conformer_body, seq=s, width=w, tr=tr),
        grid=(b,),
        in_specs=[pl.BlockSpec((1, s, w2), lambda i: (i, 0, 0)),
                  pl.BlockSpec((CONV_K, w), lambda i: (0, 0)),
                  vec, vec, vec],
        out_specs=pl.BlockSpec((1, s, w), lambda i: (i, 0, 0)),
        out_shape=jax.ShapeDtypeStruct((b, s, w), BF16),
        scratch_shapes=[pltpu.VMEM((s + 2 * CONV_PAD, w), F32)],
        compiler_params=_cparams(("parallel",)),
        name="conformer_conv",
    )(pd, conv_w, conv_b.reshape(1, w), ln_g.reshape(1, w), ln_b.reshape(1, w))


def _mix_ffn_body(x_ref, xp_ref, xn_ref, ya_ref, yap_ref, yan_ref, yb_ref, ybp_ref, ybn_ref,
                  mgate_ref, wo_ref, g_ref, sc_ref, sh_ref, gate_ref, wup_ref, cw_ref, cb_ref,
                  wd_ref, fg_ref, o_ref, act_ref, *, seq, tm, final_norm):
    rows = tm + 2 * HALO
    dff = wd_ref.shape[0]
    with_halo = lambda p, m, n: jnp.concatenate([p[0], m[0], n[0]], axis=0)
    mix = (jnp.dot(with_halo(yap_ref, ya_ref, yan_ref), wo_ref[0], preferred_element_type=F32)
           + jnp.dot(with_halo(ybp_ref, yb_ref, ybn_ref), wo_ref[1], preferred_element_type=F32))
    xa = with_halo(xp_ref, x_ref, xn_ref) + mgate_ref[0] * mix
    x_mid = xa[HALO:HALO + tm]
    hb = _modulated_norm(xa, g_ref[...], sc_ref[0], sh_ref[0]).astype(BF16)
    pos = pl.program_id(1) * tm - HALO + lax.broadcasted_iota(jnp.int32, (rows, 1), 0)
    inside = (pos >= 0) & (pos < seq)

    for lo in range(0, dff, FF_CHUNK):
        hi = lo + FF_CHUNK
        u = jnp.dot(hb, wup_ref[:, lo:hi], preferred_element_type=F32)
        u = jnp.where(inside, u, 0.0)
        v = jnp.dot(hb[HALO:HALO + tm], wup_ref[:, dff + lo:dff + hi], preferred_element_type=F32)
        u_prev = pltpu.roll(u, 1, 0)[HALO:HALO + tm]
        u_next = pltpu.roll(u, rows - 1, 0)[HALO:HALO + tm]
        uc = (cw_ref[0:1, lo:hi] * u_prev + cw_ref[1:2, lo:hi] * u[HALO:HALO + tm]
              + cw_ref[2:3, lo:hi] * u_next + cb_ref[:, lo:hi])
        act_ref[:, lo:hi] = (_silu(uc) * v).astype(BF16)
    y = x_mid + gate_ref[0] * jnp.dot(act_ref[...], wd_ref[...], preferred_element_type=F32)
    if final_norm:
        ms = jnp.mean(y * y, axis=-1, keepdims=True)
        y = y * lax.rsqrt(ms + RMS_EPS) * fg_ref[...]
    o_ref[0] = y


def _mix_ffn(x, ya, yb, mix_gate, w_out, g, scale, shift, gate, w_up, conv_w, conv_b, w_down,
             final_g, final_norm, tm):
    b, s, d = x.shape
    dff = w_down.shape[0]
    half = ya.shape[-1]
    assert dff % FF_CHUNK == 0 and s % tm == 0 and tm % HALO == 0
    hb = tm // HALO
    last = s // HALO - 1
    tiles = lambda n: [pl.BlockSpec((1, tm, n), lambda i, j: (i, j, 0)),
                       pl.BlockSpec((1, HALO, n), lambda i, j: (i, jnp.maximum(j * hb - 1, 0), 0)),
                       pl.BlockSpec((1, HALO, n), lambda i, j: (i, jnp.minimum((j + 1) * hb, last), 0))]
    row = pl.BlockSpec((1, 1, d), lambda i, j: (i, 0, 0))
    vec = pl.BlockSpec((1, d), lambda i, j: (0, 0))
    return pl.pallas_call(
        functools.partial(_mix_ffn_body, seq=s, tm=tm, final_norm=final_norm),
        grid=(b, s // tm),
        in_specs=[*tiles(d), *tiles(half), *tiles(half),
                  row, _const_spec((2, half, d)),
                  vec, row, row, row,
                  _const_spec((d, 2 * dff)),
                  _const_spec((conv_w.shape[0], dff)), _const_spec((1, dff)),
                  _const_spec((dff, d)),
                  vec],
        out_specs=pl.BlockSpec((1, tm, d), lambda i, j: (i, j, 0)),
        out_shape=jax.ShapeDtypeStruct((b, s, d), F32),
        scratch_shapes=[pltpu.VMEM((tm, dff), BF16)],
        compiler_params=_cparams(("parallel", "parallel")),
        name="mix_ffn",
    )(x, x, x, ya, ya, ya, yb, yb, yb, mix_gate, w_out.reshape(2, half, d).astype(BF16),
      g.reshape(1, d), scale, shift, gate, w_up.astype(BF16), conv_w, conv_b.reshape(1, dff),
      w_down.astype(BF16), final_g.reshape(1, d))


def kernel(x, c, ada_w, ada_b, norm_g, final_g, ab_w_in, ab_w_out, hgrn_gamma, hgrn_norm_g, rwkv_mu, rwkv_w0, rwkv_w2, rwkv_a0, rwkv_a2, rwkv_g2, rwkv_kk, rwkv_ka, rwkv_rk, rwkv_lnx_w, rwkv_lnx_b, cd_w_in, cd_w_out, dconv_w, dconv_b, dconv_ln_g, dconv_ln_b, ffn_w_up, ffn_conv_w, ffn_conv_b, ffn_w_down):
    bsz, seq, d = x.shape
    depth = ada_w.shape[0]
    a_in = 5 * (hgrn_gamma.shape[-1])
    c_width = dconv_w.shape[-1]
    tm = min(seq, 512)

    mod = _adaln(c, ada_w.reshape(depth * 2, d, 3 * d), ada_b.reshape(depth * 2, 3 * d))
    mod = mod.reshape(depth, 2, bsz, 3, 1, d)

    for l in range(depth):
        j = l // 2
        shift, scale, gate = mod[l, 0, :, 0], mod[l, 0, :, 1], mod[l, 0, :, 2]
        if l % 2 == 0:
            pa, pb = _inproj(x, norm_g[l, 0], scale, shift, ab_w_in[j],
                             (a_in, ab_w_in.shape[-1] - a_in), (F32, F32), tm)
            y1, y2 = _ab_mixers(pa, pb, l, hgrn_gamma, hgrn_norm_g[j],
                                rwkv_mu[j], rwkv_w0[j], rwkv_w2[j], rwkv_a0[j], rwkv_a2[j], rwkv_g2[j],
                                rwkv_kk[j], rwkv_ka[j], rwkv_rk[j], rwkv_lnx_w[j], rwkv_lnx_b[j])
            w_out = ab_w_out[j]
        else:
            pc, pd = _inproj(x, norm_g[l, 0], scale, shift, cd_w_in[j],
                             (cd_w_in.shape[-1] - 2 * c_width, 2 * c_width), (BF16, F32), tm)
            y1 = _fnet(pc)
            y2 = _conformer(pd, dconv_w[j], dconv_b[j], dconv_ln_g[j], dconv_ln_b[j])
            w_out = cd_w_out[j]
        shift, scale, ffn_gate = mod[l, 1, :, 0], mod[l, 1, :, 1], mod[l, 1, :, 2]
        x = _mix_ffn(x, y1, y2, gate, w_out, norm_g[l, 1], scale, shift, ffn_gate, ffn_w_up[l],
                     ffn_conv_w[l], ffn_conv_b[l], ffn_w_down[l], final_g, l == depth - 1, tm)
    return x
```
